```python
import math
import jax, jax.numpy as jnp
from jax import lax
import numpy as np

D_MODEL = 1024
BATCH = 2
SEQ = 8192
DEPTH = 4
DEC_BATCH = 128
DEC_SEQ = 8
PAST_LEN = 2048
PAGE_SIZE = 128

HEAD_DIM = 64
A_WIDTH = D_MODEL // 2
A_HEADS = A_WIDTH // HEAD_DIM
B_WIDTH = D_MODEL // 4
C_WIDTH = D_MODEL // 4
C_DK = 64
C_DV = 64
C_HEADS = C_WIDTH // C_DV
MIX_WIDTH = A_WIDTH + B_WIDTH + C_WIDTH
IDX_HEADS = 8
IDX_DIM = 64
TOPK_MAX = 256
CONV_K = 3
D_FF = 4 * D_MODEL
N_BUCKETS = 32
MAX_DISTANCE = 128
Q_BLOCK = 128
HGRN_CHUNK = 16
EPS = 1e-6
NEG_BIG = -1e30

IN_SIZES = (A_WIDTH, A_WIDTH, A_WIDTH, IDX_HEADS * IDX_DIM, IDX_DIM, IDX_HEADS,
            B_WIDTH, B_WIDTH, B_WIDTH,
            C_HEADS * C_DK, C_WIDTH, C_HEADS * C_DK, C_WIDTH)
IN_WIDTH = sum(IN_SIZES)

kernel_name = "hymba_dsa_conv_hgrn2_decode_step"


def rmsnorm(x, g):
    xf = x.astype(jnp.float32)
    y = xf * lax.rsqrt(jnp.mean(xf * xf, axis=-1, keepdims=True) + EPS)
    return (y * g.astype(jnp.float32)).astype(x.dtype)


def split_cols(p):
    out, off = [], 0
    for s in IN_SIZES:
        out.append(p[..., off:off + s])
        off += s
    return out


def t5_bucket(dist):
    dist = jnp.maximum(dist, 0)
    max_exact = N_BUCKETS // 2
    d = jnp.maximum(dist, 1).astype(jnp.float32)
    large = max_exact + (jnp.log(d / max_exact) / math.log(MAX_DISTANCE / max_exact)
                         * (N_BUCKETS - max_exact)).astype(jnp.int32)
    large = jnp.clip(large, 0, N_BUCKETS - 1)
    return jnp.where(dist < max_exact, dist, large)


def indexer_select(qi, wi, ki, tpos, topk):
    s = jnp.einsum('bqhd,bsd->bqhs', qi, ki, preferred_element_type=jnp.float32)
    score = jnp.einsum('bqhs,bqh->bqs', jax.nn.relu(s), wi.astype(jnp.float32)) * (IDX_DIM ** -0.5)
    mask = jnp.arange(ki.shape[1])[None, None, :] <= tpos[None, :, None]
    score = jnp.where(mask, score, NEG_BIG)
    return lax.top_k(score, topk)[1]


def attend_selected(q, k_sel, v_sel, idx, tpos, rel_bias):
    logits = jnp.einsum('bqhd,bqkhd->bqhk', q, k_sel, preferred_element_type=jnp.float32) * (HEAD_DIM ** -0.5)
    dist = tpos[None, :, None] - idx
    bias = jnp.moveaxis(rel_bias[t5_bucket(dist)], -1, 2).astype(jnp.float32)
    logits = jnp.where((dist >= 0)[:, :, None, :], logits + bias, NEG_BIG)
    p = jax.nn.softmax(logits, axis=-1)
    return jnp.einsum('bqhk,bqkhd->bqhd', p.astype(v_sel.dtype), v_sel)


def dsa_prompt(q, k, v, qi, ki, wi, rel_bias):
    B, T = q.shape[:2]
    topk = min(TOPK_MAX, T // 4)
    gather = jax.vmap(lambda src, ix: src[ix])

    def block(i):
        start = i * Q_BLOCK
        sl = lambda a: lax.dynamic_slice_in_dim(a, start, Q_BLOCK, axis=1)
        tpos = start + jnp.arange(Q_BLOCK)
        idx = indexer_select(sl(qi), sl(wi), ki, tpos, topk)
        return attend_selected(sl(q), gather(k, idx), gather(v, idx), idx, tpos, rel_bias)

    out = lax.map(block, jnp.arange(T // Q_BLOCK))
    return jnp.moveaxis(out, 0, 1).reshape(B, T, A_WIDTH)


def dsa_sample(q, k_new, v_new, qi, ki_new, wi, pool_k, pool_v, pool_ki, page_table, rel_bias):
    DB, DS = q.shape[:2]
    n_pages = page_table.shape[1]
    page = pool_k.shape[1]
    past = n_pages * page
    topk = min(TOPK_MAX, (past + DS) // 4)
    ki_past = pool_ki[page_table].reshape(DB, past, IDX_DIM)
    ki_all = jnp.concatenate([ki_past, ki_new], axis=1)
    tpos = past + jnp.arange(DS)
    idx = indexer_select(qi, wi, ki_all, tpos, topk)
    in_past = (idx < past)[..., None, None]
    pidx = jnp.minimum(idx, past - 1)
    phys = jax.vmap(lambda pt, p: pt[p])(page_table, pidx // page)
    rows = phys * page + pidx % page
    nidx = jnp.clip(idx - past, 0, DS - 1)
    gather = jax.vmap(lambda src, ix: src[ix])
    flat_k = pool_k.reshape(-1, A_HEADS, HEAD_DIM)
    flat_v = pool_v.reshape(-1, A_HEADS, HEAD_DIM)
    k_sel = jnp.where(in_past, flat_k[rows], gather(k_new, nidx))
    v_sel = jnp.where(in_past, flat_v[rows], gather(v_new, nidx))
    out = attend_selected(q, k_sel, v_sel, idx, tpos, rel_bias)
    return out.reshape(DB, DS, A_WIDTH)


def short_conv(bg, cg, h, conv_w, prev):
    u = cg * h
    ext = jnp.concatenate([prev.astype(u.dtype), u], axis=1)
    T = u.shape[1]
    y = sum(conv_w[j] * ext[:, j:j + T] for j in range(CONV_K))
    return bg * y, ext[:, ext.shape[1] - (CONV_K - 1):]


def gla_chunked(q, k, logf, v, S0):
    B, T, H, dk = q.shape
    C = math.gcd(T, HGRN_CHUNK)
    n = T // C
    to_chunks = lambda a: jnp.swapaxes(a.reshape(B, n, C, *a.shape[2:]), 0, 1)
    causal = jnp.tril(jnp.ones((C, C), dtype=bool))[None, :, :, None, None]

    def step(S, inp):
        qc, kc, lc, vc = inp
        b = jnp.cumsum(lc, axis=1)
        o_inter = jnp.einsum('bthk,bhkv->bthv', qc * jnp.exp(b), S)
        diff = b[:, :, None] - b[:, None, :]
        decay = jnp.where(causal, jnp.exp(jnp.where(causal, diff, 0.0)), 0.0)
        A = jnp.einsum('bthk,bshk,btshk->bhts', qc, kc, decay)
        o_intra = jnp.einsum('bhts,bshv->bthv', A, vc)
        b_last = b[:, -1]
        S_new = jnp.exp(b_last)[..., None] * S + jnp.einsum(
            'bshk,bshv->bhkv', kc * jnp.exp(b_last[:, None] - b), vc)
        return S_new, o_inter + o_intra

    S, o = lax.scan(step, S0, (to_chunks(q), to_chunks(k), to_chunks(logf), to_chunks(v)))
    return jnp.swapaxes(o, 0, 1).reshape(B, T, H, -1), S


def hgrn2(fz, i, qz, g, lb, norm_g, S0):
    B, T = fz.shape[:2]
    f32 = jnp.float32
    fz = fz.astype(f32)
    lb = lb.astype(f32)
    f = lb + (1.0 - lb) * jax.nn.sigmoid(fz)
    logf = jnp.log(jnp.maximum(f, 1e-30))
    k = (1.0 - lb) * jax.nn.sigmoid(-fz)
    q = jax.nn.silu(qz.astype(f32))
    hd = lambda a, d: a.reshape(B, T, C_HEADS, d)
    o, S = gla_chunked(hd(q, C_DK), hd(k, C_DK), hd(logf, C_DK), hd(i.astype(f32), C_DV), S0.astype(f32))
    o = o * lax.rsqrt(jnp.mean(o * o, axis=-1, keepdims=True) + EPS) * norm_g.astype(f32).reshape(C_HEADS, C_DV)
    o = o * jax.nn.silu(hd(g.astype(f32), C_DV))
    return o.reshape(B, T, C_WIDTH).astype(i.dtype), S.astype(S0.dtype)


def trunk(x, attend, conv_state, hgrn_state, w_in, w_out, conv_w, lb_logits, hg_norm_g,
          norm1_g, norm2_g, w_ff1, w_ff2, final_g):
    B, T = x.shape[:2]
    sm = jax.nn.softmax(lb_logits.astype(jnp.float32), axis=0)
    lb_all = jnp.cumsum(sm, axis=0) - sm[0]
    ks, vs, kis, cs, ss = [], [], [], [], []
    for l in range(DEPTH):
        xn = rmsnorm(x, norm1_g[l])
        aq, ak, av, iq, ik, iw, bg, cg, bh, hf, hi, hq, hg = split_cols(xn @ w_in[l])
        q = aq.reshape(B, T, A_HEADS, HEAD_DIM)
        k = ak.reshape(B, T, A_HEADS, HEAD_DIM)
        v = av.reshape(B, T, A_HEADS, HEAD_DIM)
        qi = iq.reshape(B, T, IDX_HEADS, IDX_DIM)
        wi = iw * (IDX_HEADS ** -0.5)
        a_out = attend(l, q, k, v, qi, ik, wi)
        b_out, conv_new = short_conv(bg, cg, bh, conv_w[l], conv_state[l])
        c_out, S_new = hgrn2(hf, hi, hq, hg, lb_all[l], hg_norm_g[l], hgrn_state[l])
        x = x + jnp.concatenate([a_out, b_out, c_out], axis=-1) @ w_out[l]
        h = rmsnorm(x, norm2_g[l]) @ w_ff1[l]
        x = x + jnp.square(jax.nn.relu(h)) @ w_ff2[l]
        ks.append(k); vs.append(v); kis.append(ik); cs.append(conv_new); ss.append(S_new)
    y = rmsnorm(x, final_g)
    return (y, jnp.stack(ks), jnp.stack(vs), jnp.stack(kis), jnp.stack(cs), jnp.stack(ss))


def setup_inputs(seed: int = 0) -> dict:
    key = jax.random.key(seed)
    kk = jax.random.split(key, 24)
    nrm = lambda i, shape, scale: jax.random.normal(kk[i], shape, jnp.float32) * scale
    n_pages = PAST_LEN // PAGE_SIZE
    n_used = DEC_BATCH * n_pages
    n_pool = (5 * n_used + 3) // 4
    page_table = jax.random.permutation(kk[0], n_pool)[:n_used].reshape(DEC_BATCH, n_pages).astype(jnp.int32)
    return {
        "x_prompt": nrm(1, (BATCH, SEQ, D_MODEL), 1.0),
        "x_sample": nrm(2, (DEC_BATCH, DEC_SEQ, D_MODEL), 1.0),
        "cache_k": nrm(3, (DEPTH, n_pool, PAGE_SIZE, A_HEADS, HEAD_DIM), 1.0),
        "cache_v": nrm(4, (DEPTH, n_pool, PAGE_SIZE, A_HEADS, HEAD_DIM), 1.0),
        "cache_kidx": nrm(5, (DEPTH, n_pool, PAGE_SIZE, IDX_DIM), 1.0),
        "state_conv": nrm(6, (DEPTH, DEC_BATCH, CONV_K - 1, B_WIDTH), 1.0),
        "state_hgrn": nrm(7, (DEPTH, DEC_BATCH, C_HEADS, C_DK, C_DV), 0.3),
        "page_table": page_table,
        "w_in": nrm(8, (DEPTH, D_MODEL, IN_WIDTH), D_MODEL ** -0.5),
        "w_out": nrm(9, (DEPTH, MIX_WIDTH, D_MODEL), MIX_WIDTH ** -0.5),
        "conv_w": nrm(10, (DEPTH, CONV_K, B_WIDTH), CONV_K ** -0.5),
        "hgrn_lb_logits": nrm(11, (DEPTH, C_HEADS * C_DK), 0.5),
        "hgrn_norm_g": 1.0 + nrm(12, (DEPTH, C_WIDTH), 0.02),
        "rel_bias": nrm(13, (N_BUCKETS, A_HEADS), 0.5),
        "norm1_g": 1.0 + nrm(14, (DEPTH, D_MODEL), 0.02),
        "norm2_g": 1.0 + nrm(15, (DEPTH, D_MODEL), 0.02),
        "w_ff1": nrm(16, (DEPTH, D_MODEL, D_FF), D_MODEL ** -0.5),
        "w_ff2": nrm(17, (DEPTH, D_FF, D_MODEL), D_FF ** -0.5),
        "final_g": 1.0 + nrm(18, (D_MODEL,), 0.02),
    }


def reference(x_prompt, x_sample, cache_k, cache_v, cache_kidx, state_conv, state_hgrn, page_table,
              w_in, w_out, conv_w, hgrn_lb_logits, hgrn_norm_g, rel_bias, norm1_g, norm2_g,
              w_ff1, w_ff2, final_g):
    attend_p = lambda l, q, k, v, qi, ki, wi: dsa_prompt(q, k, v, qi, ki, wi, rel_bias)
    conv0 = jnp.zeros((DEPTH, x_prompt.shape[0], CONV_K - 1, B_WIDTH), x_prompt.dtype)
    hgrn0 = jnp.zeros((DEPTH, x_prompt.shape[0], C_HEADS, C_DK, C_DV), x_prompt.dtype)
    y_prompt, kp, vp, kip, cp, sp = trunk(
        x_prompt, attend_p, conv0, hgrn0, w_in, w_out, conv_w, hgrn_lb_logits, hgrn_norm_g,
        norm1_g, norm2_g, w_ff1, w_ff2, final_g)
    attend_s = lambda l, q, k, v, qi, ki, wi: dsa_sample(
        q, k, v, qi, ki, wi, cache_k[l], cache_v[l], cache_kidx[l], page_table, rel_bias)
    y_sample, ksm, vsm, kism, csm, ssm = trunk(
        x_sample, attend_s, state_conv, state_hgrn, w_in, w_out, conv_w, hgrn_lb_logits, hgrn_norm_g,
        norm1_g, norm2_g, w_ff1, w_ff2, final_g)
    return (y_prompt, y_sample, kp, vp, kip, cp, sp, ksm, vsm, kism, csm, ssm)
```

```python
import functools
import math

import jax
import jax.numpy as jnp
from jax import lax
from jax.experimental import pallas as pl
from jax.experimental.pallas import tpu as pltpu

F32 = jnp.float32
BF16 = jnp.bfloat16
I32 = jnp.int32

HEAD_DIM = 64
A_HEADS = 8
A_WIDTH = A_HEADS * HEAD_DIM
IDX_HEADS = 8
IDX_DIM = 64
B_WIDTH = 256
C_HEADS = 4
C_DK = 64
C_DV = 64
C_WIDTH = C_HEADS * C_DV
TOPK_MAX = 256
CONV_K = 3
N_BUCKETS = 32
MAX_DISTANCE = 128
HGRN_CHUNK = 16
EPS = 1e-6
NEG_BIG = -1e30
LOG2E = 1.4426950408889634
INT_MIN = -(2 ** 31)
INT_MAX = 2 ** 31 - 1
EXP_CLAMP = 80.0

LANES = 128
SUBLANES = 8
VMEM_LIMIT = 56 * 1024 * 1024

COL_Q, COL_K, COL_V, COL_QI = 0, 512, 1024, 1536
COL_BG, COL_CG, COL_BH, COL_HF, COL_HI, COL_HQ, COL_HG = (2048 + 256 * i for i in range(7))
COL_IKW = 3840
P_WIDTH = 3968

TQ = 128
SCORE_CHUNK = 512
ATT_CHUNK = 256


def _cparams(sem):
    return pltpu.CompilerParams(dimension_semantics=sem, vmem_limit_bytes=VMEM_LIMIT)


def _nt_dot(a, b):
    return lax.dot_general(a, b, (((1,), (1,)), ((), ())), preferred_element_type=F32)


def _tn_dot(a, b):
    return lax.dot_general(a, b, (((0,), (0,)), ((), ())), preferred_element_type=F32)


def _float_key(x):
    bits = pltpu.bitcast(x, I32)
    return bits ^ ((bits >> 31) & INT_MAX)


def _inproj_kernel(x_ref, g_ref, w_ref, p_ref, kt_ref, ikt_ref, vb_ref):
    x = x_ref[...]
    ms = jnp.mean(x * x, axis=-1, keepdims=True)
    xn = (x * lax.rsqrt(ms + EPS) * g_ref[...]).astype(BF16)
    res = jnp.dot(xn, w_ref[...], preferred_element_type=F32)
    p_ref[...] = res
    kt_ref[...] = res[:, COL_K:COL_K + A_WIDTH].T.astype(BF16)
    ikt_ref[...] = res[:, COL_IKW:COL_IKW + LANES].T[:IDX_DIM].astype(BF16)
    vb_ref[...] = res[:, COL_V:COL_V + A_WIDTH].astype(BF16)


def _inproj(x, g, w, tm):
    n, d = x.shape
    return pl.pallas_call(
        _inproj_kernel,
        grid=(n // tm,),
        in_specs=[
            pl.BlockSpec((tm, d), lambda i: (i, 0)),
            pl.BlockSpec((1, d), lambda i: (0, 0)),
            pl.BlockSpec((d, P_WIDTH), lambda i: (0, 0)),
        ],
        out_specs=[
            pl.BlockSpec((tm, P_WIDTH), lambda i: (i, 0)),
            pl.BlockSpec((A_WIDTH, tm), lambda i: (0, i)),
            pl.BlockSpec((IDX_DIM, tm), lambda i: (0, i)),
            pl.BlockSpec((tm, A_WIDTH), lambda i: (i, 0)),
        ],
        out_shape=[
            jax.ShapeDtypeStruct((n, P_WIDTH), F32),
            jax.ShapeDtypeStruct((A_WIDTH, n), BF16),
            jax.ShapeDtypeStruct((IDX_DIM, n), BF16),
            jax.ShapeDtypeStruct((n, A_WIDTH), BF16),
        ],
        compiler_params=_cparams(("arbitrary",)),
        name="inproj",
    )(x, g.reshape(1, d), w)


def _outffn_kernel(x_ref, a_ref, bc_ref, wo_ref, g2_ref, w1_ref, w2_ref, gf_ref, o_ref,
                   acc_ref, hn_ref, *, final):
    j = pl.program_id(1)

    @pl.when(j == 0)
    def _():
        x1 = x_ref[...]
        x1 = x1 + jnp.dot(a_ref[...].astype(BF16), wo_ref[:A_WIDTH, :], preferred_element_type=F32)
        x1 = x1 + jnp.dot(bc_ref[...].astype(BF16), wo_ref[A_WIDTH:, :], preferred_element_type=F32)
        acc_ref[...] = x1
        ms = jnp.mean(x1 * x1, axis=-1, keepdims=True)
        hn_ref[...] = (x1 * lax.rsqrt(ms + EPS) * g2_ref[...]).astype(BF16)

    h = jnp.dot(hn_ref[...], w1_ref[...], preferred_element_type=F32)
    h = jnp.maximum(h, 0.0)
    h = (h * h).astype(BF16)
    acc_ref[...] += jnp.dot(h, w2_ref[...], preferred_element_type=F32)

    @pl.when(j == pl.num_programs(1) - 1)
    def _():
        y = acc_ref[...]
        if final:
            ms = jnp.mean(y * y, axis=-1, keepdims=True)
            y = y * lax.rsqrt(ms + EPS) * gf_ref[...]
        o_ref[...] = y


def _outffn(x, a, bc, wo, g2, w1, w2, gf, tm, tf, final):
    n, d = x.shape
    dff = w1.shape[1]
    return pl.pallas_call(
        functools.partial(_outffn_kernel, final=final),
        grid=(n // tm, dff // tf),
        in_specs=[
            pl.BlockSpec((tm, d), lambda i, j: (i, 0)),
            pl.BlockSpec((tm, A_WIDTH), lambda i, j: (i, 0)),
            pl.BlockSpec((tm, B_WIDTH + C_WIDTH), lambda i, j: (i, 0)),
            pl.BlockSpec(wo.shape, lambda i, j: (0, 0)),
            pl.BlockSpec((1, d), lambda i, j: (0, 0)),
            pl.BlockSpec((d, tf), lambda i, j: (0, j)),
            pl.BlockSpec((tf, d), lambda i, j: (j, 0)),
            pl.BlockSpec((1, d), lambda i, j: (0, 0)),
        ],
        out_specs=pl.BlockSpec((tm, d), lambda i, j: (i, 0)),
        out_shape=jax.ShapeDtypeStruct((n, d), F32),
        scratch_shapes=[pltpu.VMEM((tm, d), F32), pltpu.VMEM((tm, d), BF16)],
        compiler_params=_cparams(("arbitrary", "arbitrary")),
        name="outffn_final" if final else "outffn",
    )(x, a, bc, wo, g2.reshape(1, d), w1, w2, gf.reshape(1, d))


def _topk_threshold(count_ge, count_eq_before, rows, topk, pos_bits, pos_all):
    kf = float(topk)
    ans = jnp.where(count_ge(jnp.zeros((rows, 1), I32)) >= kf, 0, INT_MIN).astype(I32)

    def body(i, a):
        cand = a + (jnp.int32(1) << (30 - i))
        return jnp.where(count_ge(cand) >= kf, cand, a)

    ans = lax.fori_loop(0, 31, body, ans)
    need = kf - count_ge(ans + 1)

    def jbody(i, xcut):
        cand = xcut + (jnp.int32(1) << (pos_bits - 1 - i))
        return jnp.where(count_eq_before(ans, cand) < need, cand, xcut)

    jcut = lax.fori_loop(0, pos_bits, jbody, jnp.zeros((rows, 1), I32))
    jcut = jnp.where(count_ge(ans) > kf, jcut, pos_all)
    return ans, jcut


def _dsa_prompt_kernel(cfar_ref, q_ref, qi_ref, ikw_ref, kt_ref, ikt_ref, vb_ref, tab_ref, o_ref,
                       sc_ref, qh_ref, qih_ref, wb_ref, m_ref, lp_ref, acc_ref, ans_ref, jcut_ref,
                       *, seq, topk):
    qb = pl.program_id(1)
    tq = TQ
    qs = qb * tq
    w_scale = (IDX_HEADS ** -0.5) * (IDX_DIM ** -0.5)
    q_scale = (HEAD_DIM ** -0.5) * LOG2E

    q = q_ref[...]
    qi = qi_ref[...]
    wi = ikw_ref[:, IDX_DIM:IDX_DIM + IDX_HEADS] * w_scale
    for h in range(A_HEADS):
        qh_ref[h] = (q[:, h * HEAD_DIM:(h + 1) * HEAD_DIM] * q_scale).astype(BF16)
    for h in range(IDX_HEADS):
        qih_ref[h] = qi[:, h * IDX_DIM:(h + 1) * IDX_DIM].astype(BF16)
        wb_ref[h] = jnp.broadcast_to(wi[:, h:h + 1], (tq, LANES))

    tpos = qs + lax.broadcasted_iota(I32, (tq, 1), 0)

    n_sc = (qs + tq + SCORE_CHUNK - 1) // SCORE_CHUNK

    def score_body(c, carry):
        c0 = pl.multiple_of(c * SCORE_CHUNK, SCORE_CHUNK)
        ikt = ikt_ref[:, pl.ds(c0, SCORE_CHUNK)]
        score = jnp.zeros((tq, SCORE_CHUNK), F32)
        for h in range(IDX_HEADS):
            s = jnp.dot(qih_ref[h], ikt, preferred_element_type=F32)
            w = wb_ref[h]
            w = jnp.concatenate([w] * (SCORE_CHUNK // LANES), axis=1)
            score = score + jnp.maximum(s, 0.0) * w
        pos = c0 + lax.broadcasted_iota(I32, (1, SCORE_CHUNK), 1)
        score = jnp.where(pos <= tpos, score, NEG_BIG)
        sc_ref[:, pl.ds(c0, SCORE_CHUNK)] = _float_key(score)
        return carry

    lax.fori_loop(0, n_sc, score_body, 0)

    ans_ref[...] = jnp.full((tq, 1), INT_MIN + 1, I32)
    jcut_ref[...] = jnp.full((tq, 1), seq, I32)

    def lane_fold(x):
        parts = [x[:, i * LANES:(i + 1) * LANES] for i in range(SCORE_CHUNK // LANES)]
        return functools.reduce(lambda a, b: a + b, parts)

    def count_ge(cand):
        def body(c, acc):
            c0 = pl.multiple_of(c * SCORE_CHUNK, SCORE_CHUNK)
            keys = sc_ref[:, pl.ds(c0, SCORE_CHUNK)]
            return acc + lane_fold(jnp.where(keys >= cand, 1.0, 0.0))
        acc = lax.fori_loop(0, n_sc, body, jnp.zeros((tq, LANES), F32))
        return jnp.sum(acc, axis=1, keepdims=True)

    def count_eq_before(a, xcut):
        def body(c, acc):
            c0 = pl.multiple_of(c * SCORE_CHUNK, SCORE_CHUNK)
            keys = sc_ref[:, pl.ds(c0, SCORE_CHUNK)]
            pos = c0 + lax.broadcasted_iota(I32, (1, SCORE_CHUNK), 1)
            hit = jnp.where(keys == a, jnp.where(pos < xcut, 1.0, 0.0), 0.0)
            return acc + lane_fold(hit)
        acc = lax.fori_loop(0, n_sc, body, jnp.zeros((tq, LANES), F32))
        return jnp.sum(acc, axis=1, keepdims=True)

    @pl.when(qs + tq > topk)
    def _():
        pos_bits = max(1, (seq - 1).bit_length())
        ans, jcut = _topk_threshold(count_ge, count_eq_before, tq, topk, pos_bits,
                                    jnp.full((tq, 1), seq, I32))
        ans_ref[...] = ans
        jcut_ref[...] = jcut

    m_ref[...] = jnp.full(m_ref.shape, NEG_BIG, F32)
    lp_ref[...] = jnp.zeros(lp_ref.shape, F32)
    acc_ref[...] = jnp.zeros(acc_ref.shape, F32)
    ans = ans_ref[...]
    jcut = jcut_ref[...]
    lane_lo = lax.broadcasted_iota(I32, (tq, LANES), 1) < HEAD_DIM

    def attend_chunk(c0, width, limit, bias_fn, cbias_fn):
        keys = sc_ref[:, pl.ds(c0, width)]
        pos = c0 + lax.broadcasted_iota(I32, (1, width), 1)
        thr = jnp.where(pos <= jcut, ans - 1, ans)
        thr = jnp.where(pos < limit, thr, INT_MAX)
        sel = keys > thr
        for j in range(A_HEADS // 2):
            vpair = vb_ref[pl.ds(c0, width), j * LANES:(j + 1) * LANES]
            pv, alphas = [], []
            for h in (2 * j, 2 * j + 1):
                kt = kt_ref[h * HEAD_DIM:(h + 1) * HEAD_DIM, pl.ds(c0, width)]
                lg = jnp.dot(qh_ref[h], kt, preferred_element_type=F32)
                lg = bias_fn(h, lg)
                lg = jnp.where(sel, lg, NEG_BIG)
                cb = cbias_fn(h)
                m_old = m_ref[h]
                m_new = jnp.maximum(m_old, jnp.max(lg, axis=1, keepdims=True) + cb)
                alpha = jnp.exp2(m_old - m_new)
                p = jnp.exp2(lg - (m_new - cb))
                m_ref[h] = m_new
                psum = functools.reduce(
                    lambda a, b: a + b, [p[:, i * LANES:(i + 1) * LANES] for i in range(width // LANES)])
                lp_ref[h] = alpha * lp_ref[h] + psum
                pv.append(jnp.dot(p.astype(BF16), vpair, preferred_element_type=F32))
                alphas.append(alpha)
            alpha_pair = jnp.where(lane_lo, alphas[0], alphas[1])
            acc_ref[j] = acc_ref[j] * alpha_pair + jnp.where(lane_lo, pv[0], pv[1])

    near0 = jnp.maximum(qb - 1, 0) * tq

    def far_body(c, carry):
        c0 = pl.multiple_of(c * ATT_CHUNK, ATT_CHUNK)
        attend_chunk(c0, ATT_CHUNK, near0, lambda h, lg: lg, lambda h: cfar_ref[h])
        return carry

    lax.fori_loop(0, (near0 + ATT_CHUNK - 1) // ATT_CHUNK, far_body, 0)

    tsel = jnp.minimum(qb, 1)
    attend_chunk(pl.multiple_of(near0, tq), 2 * tq, seq,
                 lambda h, lg: lg + tab_ref[tsel, h], lambda h: 0.0)

    for j in range(A_HEADS // 2):
        l0 = jnp.sum(lp_ref[2 * j], axis=1, keepdims=True)
        l1 = jnp.sum(lp_ref[2 * j + 1], axis=1, keepdims=True)
        inv = jnp.where(lane_lo, 1.0 / l0, 1.0 / l1)
        o_ref[:, j * LANES:(j + 1) * LANES] = acc_ref[j] * inv


def _dsa_prompt(p, kt, ikt, vb, tab, cfar, batch, seq):
    tq = TQ
    nq = seq // tq
    topk = min(TOPK_MAX, seq // 4)
    kern = functools.partial(_dsa_prompt_kernel, seq=seq, topk=topk)
    return pl.pallas_call(
        kern,
        grid=(batch, nq),
        in_specs=[
            pl.BlockSpec(memory_space=pltpu.SMEM),
            pl.BlockSpec((tq, A_WIDTH), lambda b, i: (b * nq + i, COL_Q // A_WIDTH)),
            pl.BlockSpec((tq, A_WIDTH), lambda b, i: (b * nq + i, COL_QI // A_WIDTH)),
            pl.BlockSpec((tq, LANES), lambda b, i: (b * nq + i, COL_IKW // LANES)),
            pl.BlockSpec((A_WIDTH, seq), lambda b, i: (0, b)),
            pl.BlockSpec((IDX_DIM, seq), lambda b, i: (0, b)),
            pl.BlockSpec((seq, A_WIDTH), lambda b, i: (b, 0)),
            pl.BlockSpec(tab.shape, lambda b, i: (0, 0, 0, 0)),
        ],
        out_specs=pl.BlockSpec((tq, A_WIDTH), lambda b, i: (b * nq + i, 0)),
        out_shape=jax.ShapeDtypeStruct((batch * seq, A_WIDTH), F32),
        scratch_shapes=[
            pltpu.VMEM((tq, seq), I32),
            pltpu.VMEM((A_HEADS, tq, HEAD_DIM), BF16),
            pltpu.VMEM((IDX_HEADS, tq, IDX_DIM), BF16),
            pltpu.VMEM((IDX_HEADS, tq, LANES), F32),
            pltpu.VMEM((A_HEADS, tq, 1), F32),
            pltpu.VMEM((A_HEADS, tq, LANES), F32),
            pltpu.VMEM((A_HEADS // 2, tq, LANES), F32),
            pltpu.VMEM((tq, 1), I32),
            pltpu.VMEM((tq, 1), I32),
        ],
        compiler_params=_cparams(("arbitrary", "arbitrary")),
        name="dsa_prompt",
    )(cfar, p, p, p, kt, ikt, vb, tab)


def _dsa_sample_kernel(pt_ref, q_ref, kn_ref, vn_ref, qi_ref, ikw_ref, *rest, n_pages, page, topk):
    kidx_refs = rest[:n_pages]
    k_refs = rest[n_pages:2 * n_pages]
    v_refs = rest[2 * n_pages:3 * n_pages]
    tab_last_ref, tab_new_ref, cfar_ref, bd_ref, o_ref = rest[3 * n_pages:]
    ds = q_ref.shape[0]
    rows = A_HEADS * ds
    past = n_pages * page
    width = past + LANES
    w_scale = (IDX_HEADS ** -0.5) * (IDX_DIM ** -0.5)
    q_scale = (HEAD_DIM ** -0.5) * LOG2E

    def tile_heads(x):
        return jnp.broadcast_to(x[None], (A_HEADS,) + x.shape).reshape(rows, x.shape[1])

    def pad_rows(x):
        return jnp.concatenate([x, jnp.zeros((LANES - ds, x.shape[1]), x.dtype)], axis=0)

    qi = qi_ref[...]
    q2 = jnp.concatenate([qi[:, h * IDX_DIM:(h + 1) * IDX_DIM] for h in range(IDX_HEADS)],
                         axis=0).astype(BF16)
    wi = ikw_ref[:, IDX_DIM:IDX_DIM + IDX_HEADS] * w_scale
    wcols = [jnp.broadcast_to(wi[:, h:h + 1], (ds, LANES)) for h in range(IDX_HEADS)]

    def head_sum(s):
        out = jnp.zeros((ds, LANES), F32)
        for h in range(IDX_HEADS):
            out = out + jnp.maximum(s[h * ds:(h + 1) * ds], 0.0) * wcols[h]
        return out

    scores = [head_sum(jnp.dot(q2, kidx_refs[p][0].astype(BF16), preferred_element_type=F32))
              for p in range(n_pages)]
    ik_new = pad_rows(ikw_ref[:, :IDX_DIM]).astype(BF16)
    s_new = head_sum(_nt_dot(q2, ik_new))
    qrow = lax.broadcasted_iota(I32, (ds, LANES), 0)
    kcol = lax.broadcasted_iota(I32, (ds, LANES), 1)
    s_new = jnp.where(kcol <= qrow, s_new, NEG_BIG)
    keys = _float_key(jnp.concatenate(scores + [s_new], axis=1))
    pos = lax.broadcasted_iota(I32, (1, width), 1)

    def count_ge(cand):
        return jnp.sum(jnp.where(keys >= cand, 1.0, 0.0), axis=1, keepdims=True)

    def count_eq_before(a, xcut):
        hit = jnp.where(keys == a, jnp.where(pos < xcut, 1.0, 0.0), 0.0)
        return jnp.sum(hit, axis=1, keepdims=True)

    pos_bits = max(1, (width - 1).bit_length())
    ans, jcut = _topk_threshold(count_ge, count_eq_before, ds, topk, pos_bits,
                                jnp.full((ds, 1), width, I32))
    thr = jnp.where(pos <= jcut, ans - 1, ans)
    addmask = jnp.where(keys > thr, 0.0, NEG_BIG)

    bd = bd_ref[...]
    qbd = (tile_heads(q_ref[...] * q_scale) * bd).astype(BF16)
    logits = []
    for p in range(n_pages):
        kt = k_refs[p][0].reshape(A_WIDTH, page).astype(BF16)
        lg = jnp.dot(qbd, kt, preferred_element_type=F32)
        lg = lg + (tab_last_ref[...] if p == n_pages - 1 else cfar_ref[...])
        logits.append(lg + tile_heads(addmask[:, p * page:(p + 1) * page]))
    lg = _nt_dot(qbd, pad_rows(kn_ref[...]).astype(BF16)) + tab_new_ref[...]
    logits.append(lg + tile_heads(addmask[:, past:]))
    m = functools.reduce(jnp.maximum, [jnp.max(l, axis=1, keepdims=True) for l in logits])
    acc = jnp.zeros((rows, A_WIDTH), F32)
    lsum = jnp.zeros((rows, 1), F32)
    for p in range(n_pages + 1):
        pr = jnp.exp2(logits[p] - m)
        lsum = lsum + jnp.sum(pr, axis=1, keepdims=True)
        if p < n_pages:
            vt = v_refs[p][0].reshape(A_WIDTH, page).astype(BF16)
            acc = acc + _nt_dot(pr.astype(BF16), vt)
        else:
            acc = acc + jnp.dot(pr.astype(BF16), pad_rows(vn_ref[...]).astype(BF16),
                                preferred_element_type=F32)
    acc = acc * (1.0 / lsum) * bd
    out = acc[0:ds]
    for h in range(1, A_HEADS):
        out = out + acc[h * ds:(h + 1) * ds]
    o_ref[...] = out


def _dsa_sample(p, row0, cache_kidx, cache_k, cache_v, layer, page_table, tabs, dec_batch, dec_seq):
    n_pages = page_table.shape[1]
    page = cache_k.shape[-1]
    past = n_pages * page
    topk = min(TOPK_MAX, (past + dec_seq) // 4)
    tab_last, tab_new, cfar, bd = tabs
    rb = row0 // dec_seq
    kern = functools.partial(_dsa_sample_kernel, n_pages=n_pages, page=page, topk=topk)

    def pspec(col, w):
        return pl.BlockSpec((dec_seq, w), lambda b, pt: (rb + b, col // w))

    def page_spec(arr, pg):
        blk = (None, 1) + arr.shape[2:]
        zeros = (0,) * (arr.ndim - 2)
        return pl.BlockSpec(blk, lambda b, pt: (layer, pt[b, pg]) + zeros)

    def const_spec(a):
        return pl.BlockSpec(a.shape, lambda b, pt: (0,) * a.ndim)

    in_specs = [pspec(COL_Q, A_WIDTH), pspec(COL_K, A_WIDTH), pspec(COL_V, A_WIDTH),
                pspec(COL_QI, A_WIDTH), pspec(COL_IKW, LANES)]
    in_specs += [page_spec(cache_kidx, g) for g in range(n_pages)]
    in_specs += [page_spec(cache_k, g) for g in range(n_pages)]
    in_specs += [page_spec(cache_v, g) for g in range(n_pages)]
    in_specs += [const_spec(a) for a in (tab_last, tab_new, cfar, bd)]
    grid_spec = pltpu.PrefetchScalarGridSpec(
        num_scalar_prefetch=1,
        grid=(dec_batch,),
        in_specs=in_specs,
        out_specs=pl.BlockSpec((dec_seq, A_WIDTH), lambda b, pt: (b, 0)),
    )
    args = [p] * 5 + [cache_kidx] * n_pages + [cache_k] * n_pages + [cache_v] * n_pages
    args += [tab_last, tab_new, cfar, bd]
    return pl.pallas_call(
        kern,
        grid_spec=grid_spec,
        out_shape=jax.ShapeDtypeStruct((dec_batch * dec_seq, A_WIDTH), F32),
        compiler_params=_cparams(("arbitrary",)),
        name="dsa_sample",
    )(page_table, *args)


def _mix_kernel(bg_ref, cg_ref, bh_ref, hf_ref, hi_ref, hq_ref, hg_ref, cs_ref, s0_ref,
                cw_ref, lb_ref, ng_ref, bdk_ref, bds_ref, cm_ref, seg_ref,
                o_ref, cso_ref, so_ref,
                uext_ref, st_ref, qi_ref, qa_ref, ka_ref, ks_ref, b_ref, oo_ref, *, chunk):
    t = pl.program_id(1)
    tt = bg_ref.shape[0]
    nblk = tt // chunk
    pad = SUBLANES

    @pl.when(t == 0)
    def _():
        uext_ref[pad - (CONV_K - 1):pad, :] = cs_ref[0]
        st_ref[...] = jnp.zeros(st_ref.shape, F32)
        for h in range(C_HEADS):
            st_ref[h * C_DV:(h + 1) * C_DV, h * C_DK:(h + 1) * C_DK] = s0_ref[0, h].T

    u = cg_ref[...] * bh_ref[...]
    uext_ref[pad:pad + tt, :] = u
    y = cw_ref[CONV_K - 1:CONV_K, :] * u
    for j in range(CONV_K - 1):
        y = y + cw_ref[j:j + 1, :] * uext_ref[pad - (CONV_K - 1) + j:pad - (CONV_K - 1) + j + tt, :]
    o_ref[:, :B_WIDTH] = bg_ref[...] * y
    tail = uext_ref[pad + tt - (CONV_K - 1):pad + tt, :]
    uext_ref[pad - (CONV_K - 1):pad, :] = tail
    cso_ref[0] = tail

    lb = lb_ref[...]
    fz = hf_ref[...]
    f = lb + (1.0 - lb) * jax.nn.sigmoid(fz)
    logf = jnp.log(jnp.maximum(f, 1e-30))
    kk = (1.0 - lb) * jax.nn.sigmoid(-fz)
    qz = hq_ref[...]
    qq = qz * jax.nn.sigmoid(qz)
    rowc = lax.broadcasted_iota(I32, (tt, 1), 0) % chunk
    b = logf
    d = 1
    while d < chunk:
        b = b + jnp.where(rowc >= d, pltpu.roll(b, d, 0), 0.0)
        d *= 2
    b3 = b.reshape(nblk, chunk, C_HEADS * C_DK)
    blast3 = b3[:, chunk - 1:chunk, :]
    bmid3 = b3[:, chunk // 2:chunk // 2 + 1, :]
    blast = jnp.broadcast_to(blast3, b3.shape).reshape(tt, C_HEADS * C_DK)
    bmid = jnp.broadcast_to(bmid3, b3.shape).reshape(tt, C_HEADS * C_DK)
    b_ref[...] = b
    qi_ref[...] = qq * jnp.exp(b)
    qa_ref[...] = qq * jnp.exp(jnp.clip(b - bmid, -EXP_CLAMP, EXP_CLAMP))
    ka_ref[...] = kk * jnp.exp(jnp.clip(bmid - b, -EXP_CLAMP, EXP_CLAMP))
    ks_ref[...] = kk * jnp.exp(blast - b)

    bdk = bdk_ref[...]
    bds = bds_ref[...]
    cmask = cm_ref[...]

    def tile_rows(x):
        return jnp.broadcast_to(x[None], (C_HEADS,) + x.shape).reshape(C_HEADS * chunk, x.shape[1])

    def block_body(jb, carry):
        r0 = pl.multiple_of(jb * chunk, chunk)
        rows = pl.ds(r0, chunk)
        st = st_ref[...]
        v = hi_ref[rows, :]
        o_inter = _nt_dot(qi_ref[rows, :].astype(BF16), st.astype(BF16))
        kabd = (tile_rows(ka_ref[rows, :]) * bdk).astype(BF16)
        amat = _nt_dot(qa_ref[rows, :].astype(BF16), kabd) * cmask
        vbd = (tile_rows(v) * bdk).astype(BF16)
        o_intra = jnp.dot(amat.astype(BF16), vbd, preferred_element_type=F32)
        oo_ref[rows, :] = o_inter + o_intra
        dst = _tn_dot(v.astype(BF16), ks_ref[rows, :].astype(BF16))
        el = jnp.exp(b_ref[pl.ds(r0 + chunk - 1, 1), :])
        st_ref[...] = st * el + dst * bds
        return carry

    lax.fori_loop(0, nblk, block_body, 0)

    o = oo_ref[...]
    sq = o * o
    hi = sq.astype(BF16)
    lo = (sq - hi.astype(F32)).astype(BF16)
    seg = seg_ref[...]
    ms = jnp.dot(hi, seg, preferred_element_type=F32) + jnp.dot(lo, seg, preferred_element_type=F32)
    gz = hg_ref[...]
    o_ref[:, B_WIDTH:] = o * lax.rsqrt(ms + EPS) * ng_ref[...] * (gz * jax.nn.sigmoid(gz))

    @pl.when(t == pl.num_programs(1) - 1)
    def _():
        stt = st_ref[...].T
        for h in range(C_HEADS):
            so_ref[0, h] = stt[h * C_DK:(h + 1) * C_DK, h * C_DV:(h + 1) * C_DV]


def _mix(p, row0, nseq, seq, tt, chunk, conv_state, s0, cw, lb, ng, consts):
    bdk, bds, cmask, seg = consts
    nt = seq // tt
    rb = row0 // tt

    def pspec(col):
        return pl.BlockSpec((tt, B_WIDTH), lambda s, t: (rb + s * nt + t, col // B_WIDTH))

    def const_spec(a):
        return pl.BlockSpec(a.shape, lambda s, t: (0,) * a.ndim)

    w = C_HEADS * C_DK
    lb = lb.reshape(1, w)
    ng = ng.reshape(1, C_WIDTH)
    return pl.pallas_call(
        functools.partial(_mix_kernel, chunk=chunk),
        grid=(nseq, nt),
        in_specs=[pspec(c) for c in (COL_BG, COL_CG, COL_BH, COL_HF, COL_HI, COL_HQ, COL_HG)] + [
            pl.BlockSpec((1, CONV_K - 1, B_WIDTH), lambda s, t: (s, 0, 0)),
            pl.BlockSpec((1, C_HEADS, C_DK, C_DV), lambda s, t: (s, 0, 0, 0)),
            const_spec(cw), const_spec(lb), const_spec(ng),
            const_spec(bdk), const_spec(bds), const_spec(cmask), const_spec(seg),
        ],
        out_specs=[
            pl.BlockSpec((tt, B_WIDTH + C_WIDTH), lambda s, t: (s * nt + t, 0)),
            pl.BlockSpec((1, CONV_K - 1, B_WIDTH), lambda s, t: (s, 0, 0)),
            pl.BlockSpec((1, C_HEADS, C_DK, C_DV), lambda s, t: (s, 0, 0, 0)),
        ],
        out_shape=[
            jax.ShapeDtypeStruct((nseq * seq, B_WIDTH + C_WIDTH), F32),
            jax.ShapeDtypeStruct((nseq, CONV_K - 1, B_WIDTH), F32),
            jax.ShapeDtypeStruct((nseq, C_HEADS, C_DK, C_DV), F32),
        ],
        scratch_shapes=[
            pltpu.VMEM((tt + SUBLANES, B_WIDTH), F32),
            pltpu.VMEM((C_WIDTH, w), F32),
            pltpu.VMEM((tt, w), F32),
            pltpu.VMEM((tt, w), F32),
            pltpu.VMEM((tt, w), F32),
            pltpu.VMEM((tt, w), F32),
            pltpu.VMEM((tt, w), F32),
            pltpu.VMEM((tt, C_WIDTH), F32),
        ],
        compiler_params=_cparams(("arbitrary", "arbitrary")),
        name="mix_c%d" % chunk,
    )(p, p, p, p, p, p, p, conv_state, s0, cw, lb, ng, bdk, bds, cmask, seg)


def _mix_consts(chunk):
    w = C_HEADS * C_DK
    rh = jnp.arange(C_HEADS * chunk)[:, None] // chunk
    ch = jnp.arange(w)[None, :] // C_DK
    bdk = (rh == ch).astype(F32)
    hv = jnp.arange(C_WIDTH)[:, None] // C_DV
    bds = (hv == ch).astype(F32)
    tq = jnp.arange(chunk)[:, None]
    sk = jnp.arange(C_HEADS * chunk)[None, :] % chunk
    cmask = (sk <= tq).astype(F32)
    seg = (bds / C_DV).astype(BF16)
    return bdk, bds, cmask, seg


def _t5_bucket(dist):
    dist = jnp.maximum(dist, 0)
    max_exact = N_BUCKETS // 2
    d = jnp.maximum(dist, 1).astype(F32)
    large = max_exact + (jnp.log(d / max_exact) / math.log(MAX_DISTANCE / max_exact)
                         * (N_BUCKETS - max_exact)).astype(I32)
    large = jnp.clip(large, 0, N_BUCKETS - 1)
    return jnp.where(dist < max_exact, dist, large)


def _bias_of_dist(rel_bias, dist):
    bias = jnp.moveaxis(rel_bias.astype(F32)[_t5_bucket(dist)], -1, 0) * LOG2E
    return jnp.where((dist >= 0)[None], bias, NEG_BIG)


def _far_bias(rel_bias, min_dist):
    assert int(16 + math.log(min_dist / 16) / math.log(MAX_DISTANCE / 16) * 16) >= N_BUCKETS - 1
    return rel_bias.astype(F32)[N_BUCKETS - 1] * LOG2E


def _permute_w_in(w_in):
    sizes = (A_WIDTH, A_WIDTH, A_WIDTH, IDX_HEADS * IDX_DIM, IDX_DIM, IDX_HEADS,
             B_WIDTH, B_WIDTH, B_WIDTH, C_HEADS * C_DK, C_WIDTH, C_HEADS * C_DK, C_WIDTH)
    offs = [0]
    for s in sizes:
        offs.append(offs[-1] + s)
    cols = lambda i: w_in[..., offs[i]:offs[i + 1]]
    padw = LANES - IDX_DIM - IDX_HEADS
    pad = jnp.zeros(w_in.shape[:-1] + (padw,), w_in.dtype)
    parts = [cols(i) for i in (0, 1, 2, 3, 6, 7, 8, 9, 10, 11, 12)] + [cols(4), cols(5), pad]
    return jnp.concatenate(parts, axis=-1).astype(BF16)


def kernel(x_prompt, x_sample, cache_k, cache_v, cache_kidx, state_conv, state_hgrn, page_table,
           w_in, w_out, conv_w, hgrn_lb_logits, hgrn_norm_g, rel_bias, norm1_g, norm2_g,
           w_ff1, w_ff2, final_g):
    batch, seq, d_model = x_prompt.shape
    dec_batch, dec_seq, _ = x_sample.shape
    depth = w_in.shape[0]
    n_pool, page = cache_k.shape[1], cache_k.shape[2]
    n_pages = page_table.shape[1]
    past = n_pages * page
    n_p, n_s = batch * seq, dec_batch * dec_seq
    n = n_p + n_s
    assert dec_seq == SUBLANES and seq % (2 * TQ) == 0 and seq % SCORE_CHUNK == 0
    tm = 512 if n % 512 == 0 else 256
    assert n % tm == 0 and n_p % tm == 0

    x = jnp.concatenate([x_prompt.reshape(n_p, d_model), x_sample.reshape(n_s, d_model)], axis=0)
    w_in_p = _permute_w_in(w_in)
    w_out_b = w_out.astype(BF16)
    w_ff1_b = w_ff1.astype(BF16)
    w_ff2_b = w_ff2.astype(BF16)
    sm = jax.nn.softmax(hgrn_lb_logits.astype(F32), axis=0)
    lb_all = jnp.cumsum(sm, axis=0) - sm[0]

    ck = jnp.transpose(cache_k, (0, 1, 3, 4, 2))
    cv = jnp.transpose(cache_v, (0, 1, 3, 4, 2))
    cki = jnp.transpose(cache_kidx, (0, 1, 3, 2))

    qi_ = jnp.arange(TQ)[:, None]
    kj_ = jnp.arange(2 * TQ)[None, :]
    tab_p = jnp.stack([_bias_of_dist(rel_bias, qi_ - kj_), _bias_of_dist(rel_bias, TQ + qi_ - kj_)])
    cfar_p = _far_bias(rel_bias, TQ + 1)
    rows = A_HEADS * dec_seq
    si = jnp.arange(dec_seq)[:, None]
    sj = jnp.arange(LANES)[None, :]
    tab_last = _bias_of_dist(rel_bias, page + si - sj).reshape(rows, LANES)
    d_new = jnp.where(sj < dec_seq, si - sj, -1)
    tab_new = _bias_of_dist(rel_bias, d_new).reshape(rows, LANES)
    cfar_s = jnp.broadcast_to(jnp.repeat(_far_bias(rel_bias, page + 1), dec_seq)[:, None], (rows, LANES))
    bd_s = (jnp.arange(rows)[:, None] // dec_seq == jnp.arange(A_WIDTH)[None, :] // HEAD_DIM).astype(F32)
    tabs_s = (tab_last, tab_new, cfar_s, bd_s)

    chunk_p = math.gcd(seq, HGRN_CHUNK)
    chunk_s = math.gcd(dec_seq, HGRN_CHUNK)
    tt_p = 512 if seq % 512 == 0 else seq
    consts_p = _mix_consts(chunk_p)
    consts_s = _mix_consts(chunk_s)
    conv0 = jnp.zeros((batch, CONV_K - 1, B_WIDTH), F32)
    hgrn0 = jnp.zeros((batch, C_HEADS, C_DK, C_DV), F32)

    ks, vs, kis, cps, sps, css, sss = [], [], [], [], [], [], []
    for l in range(depth):
        p, kt, ikt, vb = _inproj(x, norm1_g[l], w_in_p[l], tm)
        a_p = _dsa_prompt(p, kt, ikt, vb, tab_p, cfar_p, batch, seq)
        a_s = _dsa_sample(p, n_p, cki, ck, cv, l, page_table, tabs_s, dec_batch, dec_seq)
        bc_p, conv_p, s_p = _mix(p, 0, batch, seq, tt_p, chunk_p, conv0, hgrn0,
                                 conv_w[l], lb_all[l], hgrn_norm_g[l], consts_p)
        bc_s, conv_s, s_s = _mix(p, n_p, dec_batch, dec_seq, dec_seq, chunk_s, state_conv[l],
                                 state_hgrn[l], conv_w[l], lb_all[l], hgrn_norm_g[l], consts_s)
        a = jnp.concatenate([a_p, a_s], axis=0)
        bc = jnp.concatenate([bc_p, bc_s], axis=0)
        x = _outffn(x, a, bc, w_out_b[l], norm2_g[l], w_ff1_b[l], w_ff2_b[l], final_g,
                    tm, 1024, l == depth - 1)
        ks.append(p[:, COL_K:COL_K + A_WIDTH])
        vs.append(p[:, COL_V:COL_V + A_WIDTH])
        kis.append(p[:, COL_IKW:COL_IKW + IDX_DIM])
        cps.append(conv_p); sps.append(s_p); css.append(conv_s); sss.append(s_s)

    kall, vall, kiall = jnp.stack(ks), jnp.stack(vs), jnp.stack(kis)
    hd = lambda a, lo, hi, shp: a[:, lo:hi].reshape((depth,) + shp)
    return (
        x[:n_p].reshape(batch, seq, d_model),
        x[n_p:].reshape(dec_batch, dec_seq, d_model),
        hd(kall, 0, n_p, (batch, seq, A_HEADS, HEAD_DIM)),
        hd(vall, 0, n_p, (batch, seq, A_HEADS, HEAD_DIM)),
        hd(kiall, 0, n_p, (batch, seq, IDX_DIM)),
        jnp.stack(cps), jnp.stack(sps),
        hd(kall, n_p, n, (dec_batch, dec_seq, A_HEADS, HEAD_DIM)),
        hd(vall, n_p, n, (dec_batch, dec_seq, A_HEADS, HEAD_DIM)),
        hd(kiall, n_p, n, (dec_batch, dec_seq, IDX_DIM)),
        jnp.stack(css), jnp.stack(sss),
    )
```

```python
import functools
import math

import jax
import jax.numpy as jnp
from jax import lax
from jax.experimental import pallas as pl
from jax.experimental.pallas import tpu as pltpu

F32 = jnp.float32
BF16 = jnp.bfloat16
I32 = jnp.int32

HEAD_DIM = 64
A_HEADS = 8
A_WIDTH = A_HEADS * HEAD_DIM
IDX_HEADS = 8
IDX_DIM = 64
B_WIDTH = 256
C_HEADS = 4
C_DK = 64
C_DV = 64
C_WIDTH = C_HEADS * C_DV
TOPK_MAX = 256
CONV_K = 3
N_BUCKETS = 32
MAX_DISTANCE = 128
HGRN_CHUNK = 16
EPS = 1e-6
NEG_BIG = -1e30
LOG2E = 1.4426950408889634
INT_MIN = -(2 ** 31)
INT_MAX = 2 ** 31 - 1
EXP_CLAMP = 80.0

LANES = 128
SUBLANES = 8
VMEM_LIMIT = 56 * 1024 * 1024

COL_Q, COL_K, COL_V, COL_QI = 0, 512, 1024, 1536
COL_BG, COL_CG, COL_BH, COL_HF, COL_HI, COL_HQ, COL_HG = (2048 + 256 * i for i in range(7))
COL_IKW = 3840
P_WIDTH = 3968

TQ = 128
SCORE_CHUNK = 512
ATT_CHUNK = 512


def _cparams(sem):
    return pltpu.CompilerParams(dimension_semantics=sem, vmem_limit_bytes=VMEM_LIMIT)


def _nt_dot(a, b):
    return lax.dot_general(a, b, (((1,), (1,)), ((), ())), preferred_element_type=F32)


def _tn_dot(a, b):
    return lax.dot_general(a, b, (((0,), (0,)), ((), ())), preferred_element_type=F32)


def _float_key(x):
    bits = pltpu.bitcast(x, I32)
    return bits ^ ((bits >> 31) & INT_MAX)


def _inproj_kernel(x_ref, g_ref, w_ref, p_ref, qt_ref, qit_ref, kb_ref, vt_ref, ikw_ref, iwt_ref):
    x = x_ref[...]
    ms = jnp.mean(x * x, axis=-1, keepdims=True)
    xn = (x * lax.rsqrt(ms + EPS) * g_ref[...]).astype(BF16)
    res = jnp.dot(xn, w_ref[...], preferred_element_type=F32)
    p_ref[...] = res
    q_scale = (HEAD_DIM ** -0.5) * LOG2E
    w_scale = (IDX_HEADS ** -0.5) * (IDX_DIM ** -0.5)
    qt_ref[...] = (res[:, COL_Q:COL_Q + A_WIDTH] * q_scale).T.astype(BF16)
    qit_ref[...] = res[:, COL_QI:COL_QI + A_WIDTH].T.astype(BF16)
    kb_ref[...] = res[:, COL_K:COL_K + A_WIDTH].astype(BF16)
    vt_ref[...] = res[:, COL_V:COL_V + A_WIDTH].T.astype(BF16)
    ikw = res[:, COL_IKW:COL_IKW + LANES]
    ikw_ref[...] = ikw.astype(BF16)
    iwt_ref[...] = ikw.T[IDX_DIM:IDX_DIM + IDX_HEADS] * w_scale


def _inproj(x, g, w, tm):
    n, d = x.shape
    return pl.pallas_call(
        _inproj_kernel,
        grid=(n // tm,),
        in_specs=[
            pl.BlockSpec((tm, d), lambda i: (i, 0)),
            pl.BlockSpec((1, d), lambda i: (0, 0)),
            pl.BlockSpec((d, P_WIDTH), lambda i: (0, 0)),
        ],
        out_specs=[
            pl.BlockSpec((tm, P_WIDTH), lambda i: (i, 0)),
            pl.BlockSpec((A_WIDTH, tm), lambda i: (0, i)),
            pl.BlockSpec((A_WIDTH, tm), lambda i: (0, i)),
            pl.BlockSpec((tm, A_WIDTH), lambda i: (i, 0)),
            pl.BlockSpec((A_WIDTH, tm), lambda i: (0, i)),
            pl.BlockSpec((tm, LANES), lambda i: (i, 0)),
            pl.BlockSpec((IDX_HEADS, tm), lambda i: (0, i)),
        ],
        out_shape=[
            jax.ShapeDtypeStruct((n, P_WIDTH), F32),
            jax.ShapeDtypeStruct((A_WIDTH, n), BF16),
            jax.ShapeDtypeStruct((A_WIDTH, n), BF16),
            jax.ShapeDtypeStruct((n, A_WIDTH), BF16),
            jax.ShapeDtypeStruct((A_WIDTH, n), BF16),
            jax.ShapeDtypeStruct((n, LANES), BF16),
            jax.ShapeDtypeStruct((IDX_HEADS, n), F32),
        ],
        compiler_params=_cparams(("arbitrary",)),
        name="inproj",
    )(x, g.reshape(1, d), w)


def _outffn_kernel(x_ref, a_ref, bc_ref, wo_ref, g2_ref, w1_ref, w2_ref, gf_ref, o_ref,
                   acc_ref, hn_ref, *, final):
    j = pl.program_id(1)

    @pl.when(j == 0)
    def _():
        x1 = x_ref[...]
        x1 = x1 + jnp.dot(a_ref[...].astype(BF16), wo_ref[:A_WIDTH, :], preferred_element_type=F32)
        x1 = x1 + jnp.dot(bc_ref[...].astype(BF16), wo_ref[A_WIDTH:, :], preferred_element_type=F32)
        acc_ref[...] = x1
        ms = jnp.mean(x1 * x1, axis=-1, keepdims=True)
        hn_ref[...] = (x1 * lax.rsqrt(ms + EPS) * g2_ref[...]).astype(BF16)

    h = jnp.dot(hn_ref[...], w1_ref[...], preferred_element_type=F32)
    h = jnp.maximum(h, 0.0)
    h = (h * h).astype(BF16)
    acc_ref[...] += jnp.dot(h, w2_ref[...], preferred_element_type=F32)

    @pl.when(j == pl.num_programs(1) - 1)
    def _():
        y = acc_ref[...]
        if final:
            ms = jnp.mean(y * y, axis=-1, keepdims=True)
            y = y * lax.rsqrt(ms + EPS) * gf_ref[...]
        o_ref[...] = y


def _outffn(x, a, bc, wo, g2, w1, w2, gf, tm, tf, final):
    n, d = x.shape
    dff = w1.shape[1]
    return pl.pallas_call(
        functools.partial(_outffn_kernel, final=final),
        grid=(n // tm, dff // tf),
        in_specs=[
            pl.BlockSpec((tm, d), lambda i, j: (i, 0)),
            pl.BlockSpec((tm, A_WIDTH), lambda i, j: (i, 0)),
            pl.BlockSpec((tm, B_WIDTH + C_WIDTH), lambda i, j: (i, 0)),
            pl.BlockSpec(wo.shape, lambda i, j: (0, 0)),
            pl.BlockSpec((1, d), lambda i, j: (0, 0)),
            pl.BlockSpec((d, tf), lambda i, j: (0, j)),
            pl.BlockSpec((tf, d), lambda i, j: (j, 0)),
            pl.BlockSpec((1, d), lambda i, j: (0, 0)),
        ],
        out_specs=pl.BlockSpec((tm, d), lambda i, j: (i, 0)),
        out_shape=jax.ShapeDtypeStruct((n, d), F32),
        scratch_shapes=[pltpu.VMEM((tm, d), F32), pltpu.VMEM((tm, d), BF16)],
        compiler_params=_cparams(("arbitrary", "arbitrary")),
        name="outffn_final" if final else "outffn",
    )(x, a, bc, wo, g2.reshape(1, d), w1, w2, gf.reshape(1, d))


def _topk_threshold(count_ge, count_eq_before, rows, topk, pos_bits, pos_all):
    kf = float(topk)
    ans = _bisect_keys(count_ge, jnp.zeros((rows, 1), I32), topk)
    need = kf - count_ge(ans + 1)

    def jbody(i, xcut):
        cand = xcut + (jnp.int32(1) << (pos_bits - 1 - i))
        return jnp.where(count_eq_before(ans, cand) < need, cand, xcut)

    jcut = lax.fori_loop(0, pos_bits, jbody, jnp.zeros((rows, 1), I32))
    jcut = jnp.where(count_ge(ans) > kf, jcut, pos_all)
    return ans, jcut


def _bisect_keys(count_ge, like, topk):
    kf = float(topk)
    ans = jnp.where(count_ge(jnp.zeros_like(like)) >= kf, 0, INT_MIN).astype(I32)

    def body(i, a):
        cand = a + (jnp.int32(1) << (30 - i))
        return jnp.where(count_ge(cand) >= kf, cand, a)

    return lax.fori_loop(0, 31, body, ans)


def _dsa_prompt_kernel(cfar_ref, iwt_ref, qt_ref, qit_ref, kb_ref, ikw_ref, vt_ref, tab_ref, o_ref,
                       sc_ref, qbd_ref, qibd_ref, acc0_ref, acc1_ref, acc2_ref, acc3_ref,
                       ans_ref, jcut_ref, *, seq, topk):
    qb = pl.program_id(1)
    tq = TQ
    qs = qb * tq
    npair = A_HEADS // 2

    zq = jnp.zeros((HEAD_DIM, tq), BF16)
    for j in range(npair):
        top = qt_ref[(2 * j) * HEAD_DIM:(2 * j + 1) * HEAD_DIM, :]
        bot = qt_ref[(2 * j + 1) * HEAD_DIM:(2 * j + 2) * HEAD_DIM, :]
        qbd_ref[j] = jnp.concatenate([jnp.concatenate([top, zq], axis=1),
                                      jnp.concatenate([zq, bot], axis=1)], axis=0)
        itop = qit_ref[(2 * j) * IDX_DIM:(2 * j + 1) * IDX_DIM, :]
        ibot = qit_ref[(2 * j + 1) * IDX_DIM:(2 * j + 2) * IDX_DIM, :]
        qibd_ref[j] = jnp.concatenate([jnp.concatenate([itop, ibot], axis=1),
                                       jnp.zeros((LANES - IDX_DIM, 2 * tq), BF16)], axis=0)

    wrow = iwt_ref[...]
    tpos = qs + lax.broadcasted_iota(I32, (1, tq), 1)

    def fold(x, op):
        while x.shape[0] > SUBLANES:
            half = x.shape[0] // 2
            x = op(x[:half], x[half:])
        return x

    n_sc = (qs + tq + SCORE_CHUNK - 1) // SCORE_CHUNK

    def score_body(c, carry):
        c0 = pl.multiple_of(c * SCORE_CHUNK, SCORE_CHUNK)
        ikc = ikw_ref[pl.ds(c0, SCORE_CHUNK), :]
        score = jnp.zeros((SCORE_CHUNK, tq), F32)
        for j in range(npair):
            s = jnp.dot(ikc, qibd_ref[j], preferred_element_type=F32)
            score = score + jnp.maximum(s[:, :tq], 0.0) * wrow[2 * j:2 * j + 1, :]
            score = score + jnp.maximum(s[:, tq:], 0.0) * wrow[2 * j + 1:2 * j + 2, :]
        pos = c0 + lax.broadcasted_iota(I32, (SCORE_CHUNK, tq), 0)
        sc_ref[pl.ds(c0, SCORE_CHUNK), :] = _float_key(jnp.where(pos <= tpos, score, NEG_BIG))
        return carry

    lax.fori_loop(0, n_sc, score_body, 0)

    def count_ge(cand):
        def body(c, acc):
            c0 = pl.multiple_of(c * SCORE_CHUNK, SCORE_CHUNK)
            keys = sc_ref[pl.ds(c0, SCORE_CHUNK), :]
            return acc + fold(jnp.where(keys >= cand, 1.0, 0.0), jnp.add)
        acc = lax.fori_loop(0, n_sc, body, jnp.zeros((SUBLANES, tq), F32))
        return jnp.sum(acc, axis=0, keepdims=True)

    def count_eq_before(a, xcut):
        def body(c, acc):
            c0 = pl.multiple_of(c * SCORE_CHUNK, SCORE_CHUNK)
            keys = sc_ref[pl.ds(c0, SCORE_CHUNK), :]
            pos = c0 + lax.broadcasted_iota(I32, (SCORE_CHUNK, tq), 0)
            hit = jnp.where(keys == a, jnp.where(pos < xcut, 1.0, 0.0), 0.0)
            return acc + fold(hit, jnp.add)
        acc = lax.fori_loop(0, n_sc, body, jnp.zeros((SUBLANES, tq), F32))
        return jnp.sum(acc, axis=0, keepdims=True)

    ans_ref[...] = jnp.full((1, tq), INT_MIN + 1, I32)
    jcut_ref[...] = jnp.full((1, tq), seq, I32)

    @pl.when(qs + tq > topk)
    def _():
        kf = float(topk)
        ans = _bisect_keys(count_ge, jnp.zeros((1, tq), I32), topk)
        c_ans = count_ge(ans)

        def tie_cut(_):
            need = kf - count_ge(ans + 1)
            pos_bits = max(1, (seq - 1).bit_length())

            def jbody(i, xcut):
                cand = xcut + (jnp.int32(1) << (pos_bits - 1 - i))
                return jnp.where(count_eq_before(ans, cand) < need, cand, xcut)

            cut = lax.fori_loop(0, pos_bits, jbody, jnp.zeros((1, tq), I32))
            return jnp.where(c_ans > kf, cut, seq)

        jcut = lax.cond(jnp.max(c_ans) > kf, tie_cut, lambda _: jnp.full((1, tq), seq, I32), 0)
        ans_ref[...] = ans
        jcut_ref[...] = jcut

    acc_refs = (acc0_ref, acc1_ref, acc2_ref, acc3_ref)
    for r in acc_refs:
        r[...] = jnp.zeros(r.shape, F32)
    ans = ans_ref[...]
    jcut = jcut_ref[...]

    def attend_chunk(c0, width, limit, tsel, ms, lss):
        keys = sc_ref[pl.ds(c0, width), :]
        pos = c0 + lax.broadcasted_iota(I32, (width, tq), 0)
        thr = jnp.where(pos <= jcut, ans - 1, ans)
        thr = jnp.where(pos < limit, thr, INT_MAX)
        sel = keys > thr
        lgs = [jnp.dot(kb_ref[pl.ds(c0, width), j * LANES:(j + 1) * LANES], qbd_ref[j],
                       preferred_element_type=F32) for j in range(npair)]
        ms_new, lss_new = [], []
        for j in range(npair):
            ps, alphas = [], []
            for half in range(2):
                h = 2 * j + half
                lh = lgs[j][:, half * tq:(half + 1) * tq]
                if tsel is None:
                    cb = cfar_ref[h]
                else:
                    lh = lh + tab_ref[tsel, h]
                    cb = 0.0
                lh = jnp.where(sel, lh, NEG_BIG)
                cmax = jnp.max(fold(lh, jnp.maximum), axis=0, keepdims=True)
                m_new = jnp.maximum(ms[h], cmax + cb)
                alpha = jnp.exp2(ms[h] - m_new)
                p = jnp.exp2(lh - (m_new - cb))
                ms_new.append(m_new)
                lss_new.append(alpha * lss[h] + fold(p, jnp.add))
                ps.append(p.astype(BF16))
                alphas.append(alpha)
            vt = vt_ref[j * LANES:(j + 1) * LANES, pl.ds(c0, width)]
            pv = jnp.dot(vt, jnp.concatenate(ps, axis=1), preferred_element_type=F32)
            acc_refs[j][...] = acc_refs[j][...] * jnp.concatenate(alphas, axis=1) + pv
        return tuple(ms_new), tuple(lss_new)

    near0 = jnp.maximum(qb - 1, 0) * tq

    def far_body(c, carry):
        return attend_chunk(pl.multiple_of(c * ATT_CHUNK, ATT_CHUNK), ATT_CHUNK, near0, None, *carry)

    init = (tuple(jnp.full((1, tq), NEG_BIG, F32) for _ in range(A_HEADS)),
            tuple(jnp.zeros((SUBLANES, tq), F32) for _ in range(A_HEADS)))
    carry = lax.fori_loop(0, (near0 + ATT_CHUNK - 1) // ATT_CHUNK, far_body, init)
    _, lss = attend_chunk(pl.multiple_of(near0, tq), 2 * tq, seq, jnp.minimum(qb, 1), *carry)

    for j in range(npair):
        outs = []
        for half in range(2):
            inv = 1.0 / jnp.sum(lss[2 * j + half], axis=0, keepdims=True)
            blk = acc_refs[j][half * HEAD_DIM:(half + 1) * HEAD_DIM, half * tq:(half + 1) * tq]
            outs.append((blk * inv).T)
        o_ref[:, j * LANES:(j + 1) * LANES] = jnp.concatenate(outs, axis=1)


def _dsa_prompt(iwt, qt, qit, kb, ikwb, vt, tab, cfar, batch, seq):
    tq = TQ
    nq = seq // tq
    topk = min(TOPK_MAX, seq // 4)
    kern = functools.partial(_dsa_prompt_kernel, seq=seq, topk=topk)
    return pl.pallas_call(
        kern,
        grid=(batch, nq),
        in_specs=[
            pl.BlockSpec(memory_space=pltpu.SMEM),
            pl.BlockSpec((IDX_HEADS, tq), lambda b, i: (0, b * nq + i)),
            pl.BlockSpec((A_WIDTH, tq), lambda b, i: (0, b * nq + i)),
            pl.BlockSpec((A_WIDTH, tq), lambda b, i: (0, b * nq + i)),
            pl.BlockSpec((seq, A_WIDTH), lambda b, i: (b, 0)),
            pl.BlockSpec((seq, LANES), lambda b, i: (b, 0)),
            pl.BlockSpec((A_WIDTH, seq), lambda b, i: (0, b)),
            pl.BlockSpec(tab.shape, lambda b, i: (0, 0, 0, 0)),
        ],
        out_specs=pl.BlockSpec((tq, A_WIDTH), lambda b, i: (b * nq + i, 0)),
        out_shape=jax.ShapeDtypeStruct((batch * seq, A_WIDTH), F32),
        scratch_shapes=[
            pltpu.VMEM((seq, tq), I32),
            pltpu.VMEM((A_HEADS // 2, 2 * HEAD_DIM, 2 * tq), BF16),
            pltpu.VMEM((IDX_HEADS // 2, LANES, 2 * tq), BF16),
            pltpu.VMEM((2 * HEAD_DIM, 2 * tq), F32),
            pltpu.VMEM((2 * HEAD_DIM, 2 * tq), F32),
            pltpu.VMEM((2 * HEAD_DIM, 2 * tq), F32),
            pltpu.VMEM((2 * HEAD_DIM, 2 * tq), F32),
            pltpu.VMEM((1, tq), I32),
            pltpu.VMEM((1, tq), I32),
        ],
        compiler_params=_cparams(("arbitrary", "arbitrary")),
        name="dsa_prompt",
    )(cfar, iwt, qt, qit, kb, ikwb, vt, tab)


def _dsa_sample_kernel(pt_ref, q_ref, kn_ref, vn_ref, qi_ref, ikw_ref, *rest, n_pages, page, topk):
    kidx_refs = rest[:n_pages]
    k_refs = rest[n_pages:2 * n_pages]
    v_refs = rest[2 * n_pages:3 * n_pages]
    tab_last_ref, tab_new_ref, cfar_ref, bd_ref, o_ref = rest[3 * n_pages:]
    ds = q_ref.shape[0]
    rows = A_HEADS * ds
    past = n_pages * page
    width = past + LANES
    w_scale = (IDX_HEADS ** -0.5) * (IDX_DIM ** -0.5)
    q_scale = (HEAD_DIM ** -0.5) * LOG2E

    def tile_heads(x):
        return jnp.broadcast_to(x[None], (A_HEADS,) + x.shape).reshape(rows, x.shape[1])

    def pad_rows(x):
        return jnp.concatenate([x, jnp.zeros((LANES - ds, x.shape[1]), x.dtype)], axis=0)

    qi = qi_ref[...]
    q2 = jnp.concatenate([qi[:, h * IDX_DIM:(h + 1) * IDX_DIM] for h in range(IDX_HEADS)],
                         axis=0).astype(BF16)
    wi = ikw_ref[:, IDX_DIM:IDX_DIM + IDX_HEADS] * w_scale
    wcols = [jnp.broadcast_to(wi[:, h:h + 1], (ds, LANES)) for h in range(IDX_HEADS)]

    def head_sum(s):
        out = jnp.zeros((ds, LANES), F32)
        for h in range(IDX_HEADS):
            out = out + jnp.maximum(s[h * ds:(h + 1) * ds], 0.0) * wcols[h]
        return out

    scores = [head_sum(jnp.dot(q2, kidx_refs[p][0].astype(BF16), preferred_element_type=F32))
              for p in range(n_pages)]
    ik_new = pad_rows(ikw_ref[:, :IDX_DIM]).astype(BF16)
    s_new = head_sum(_nt_dot(q2, ik_new))
    qrow = lax.broadcasted_iota(I32, (ds, LANES), 0)
    kcol = lax.broadcasted_iota(I32, (ds, LANES), 1)
    s_new = jnp.where(kcol <= qrow, s_new, NEG_BIG)
    keys = _float_key(jnp.concatenate(scores + [s_new], axis=1))
    pos = lax.broadcasted_iota(I32, (1, width), 1)

    def count_ge(cand):
        return jnp.sum(jnp.where(keys >= cand, 1.0, 0.0), axis=1, keepdims=True)

    def count_eq_before(a, xcut):
        hit = jnp.where(keys == a, jnp.where(pos < xcut, 1.0, 0.0), 0.0)
        return jnp.sum(hit, axis=1, keepdims=True)

    pos_bits = max(1, (width - 1).bit_length())
    ans, jcut = _topk_threshold(count_ge, count_eq_before, ds, topk, pos_bits,
                                jnp.full((ds, 1), width, I32))
    thr = jnp.where(pos <= jcut, ans - 1, ans)
    addmask = jnp.where(keys > thr, 0.0, NEG_BIG)

    bd = bd_ref[...]
    qbd = (tile_heads(q_ref[...] * q_scale) * bd).astype(BF16)
    logits = []
    for p in range(n_pages):
        kt = k_refs[p][0].reshape(A_WIDTH, page).astype(BF16)
        lg = jnp.dot(qbd, kt, preferred_element_type=F32)
        lg = lg + (tab_last_ref[...] if p == n_pages - 1 else cfar_ref[...])
        logits.append(lg + tile_heads(addmask[:, p * page:(p + 1) * page]))
    lg = _nt_dot(qbd, pad_rows(kn_ref[...]).astype(BF16)) + tab_new_ref[...]
    logits.append(lg + tile_heads(addmask[:, past:]))
    m = functools.reduce(jnp.maximum, [jnp.max(l, axis=1, keepdims=True) for l in logits])
    acc = jnp.zeros((rows, A_WIDTH), F32)
    lsum = jnp.zeros((rows, 1), F32)
    for p in range(n_pages + 1):
        pr = jnp.exp2(logits[p] - m)
        lsum = lsum + jnp.sum(pr, axis=1, keepdims=True)
        if p < n_pages:
            vt = v_refs[p][0].reshape(A_WIDTH, page).astype(BF16)
            acc = acc + _nt_dot(pr.astype(BF16), vt)
        else:
            acc = acc + jnp.dot(pr.astype(BF16), pad_rows(vn_ref[...]).astype(BF16),
                                preferred_element_type=F32)
    acc = acc * (1.0 / lsum) * bd
    out = acc[0:ds]
    for h in range(1, A_HEADS):
        out = out + acc[h * ds:(h + 1) * ds]
    o_ref[...] = out


def _dsa_sample(p, row0, cache_kidx, cache_k, cache_v, layer, page_table, tabs, dec_batch, dec_seq):
    n_pages = page_table.shape[1]
    page = cache_k.shape[-1]
    past = n_pages * page
    topk = min(TOPK_MAX, (past + dec_seq) // 4)
    tab_last, tab_new, cfar, bd = tabs
    rb = row0 // dec_seq
    kern = functools.partial(_dsa_sample_kernel, n_pages=n_pages, page=page, topk=topk)

    def pspec(col, w):
        return pl.BlockSpec((dec_seq, w), lambda b, pt: (rb + b, col // w))

    def page_spec(arr, pg):
        blk = (None, 1) + arr.shape[2:]
        zeros = (0,) * (arr.ndim - 2)
        return pl.BlockSpec(blk, lambda b, pt: (layer, pt[b, pg]) + zeros)

    def const_spec(a):
        return pl.BlockSpec(a.shape, lambda b, pt: (0,) * a.ndim)

    in_specs = [pspec(COL_Q, A_WIDTH), pspec(COL_K, A_WIDTH), pspec(COL_V, A_WIDTH),
                pspec(COL_QI, A_WIDTH), pspec(COL_IKW, LANES)]
    in_specs += [page_spec(cache_kidx, g) for g in range(n_pages)]
    in_specs += [page_spec(cache_k, g) for g in range(n_pages)]
    in_specs += [page_spec(cache_v, g) for g in range(n_pages)]
    in_specs += [const_spec(a) for a in (tab_last, tab_new, cfar, bd)]
    grid_spec = pltpu.PrefetchScalarGridSpec(
        num_scalar_prefetch=1,
        grid=(dec_batch,),
        in_specs=in_specs,
        out_specs=pl.BlockSpec((dec_seq, A_WIDTH), lambda b, pt: (b, 0)),
    )
    args = [p] * 5 + [cache_kidx] * n_pages + [cache_k] * n_pages + [cache_v] * n_pages
    args += [tab_last, tab_new, cfar, bd]
    return pl.pallas_call(
        kern,
        grid_spec=grid_spec,
        out_shape=jax.ShapeDtypeStruct((dec_batch * dec_seq, A_WIDTH), F32),
        compiler_params=_cparams(("arbitrary",)),
        name="dsa_sample",
    )(page_table, *args)


def _mix_kernel(bg_ref, cg_ref, bh_ref, hf_ref, hi_ref, hq_ref, hg_ref, cs_ref, s0_ref,
                cw_ref, lb_ref, ng_ref, bdk_ref, bds_ref, cm_ref, seg_ref,
                o_ref, cso_ref, so_ref,
                uext_ref, st_ref, qi_ref, qa_ref, ka_ref, ks_ref, b_ref, oo_ref, *, chunk):
    t = pl.program_id(1)
    tt = bg_ref.shape[0]
    nblk = tt // chunk
    pad = SUBLANES

    @pl.when(t == 0)
    def _():
        uext_ref[pad - (CONV_K - 1):pad, :] = cs_ref[0]
        st_ref[...] = jnp.zeros(st_ref.shape, F32)
        for h in range(C_HEADS):
            st_ref[h * C_DV:(h + 1) * C_DV, h * C_DK:(h + 1) * C_DK] = s0_ref[0, h].T

    u = cg_ref[...] * bh_ref[...]
    uext_ref[pad:pad + tt, :] = u
    y = cw_ref[CONV_K - 1:CONV_K, :] * u
    for j in range(CONV_K - 1):
        y = y + cw_ref[j:j + 1, :] * uext_ref[pad - (CONV_K - 1) + j:pad - (CONV_K - 1) + j + tt, :]
    o_ref[:, :B_WIDTH] = bg_ref[...] * y
    tail = uext_ref[pad + tt - (CONV_K - 1):pad + tt, :]
    uext_ref[pad - (CONV_K - 1):pad, :] = tail
    cso_ref[0] = tail

    lb = lb_ref[...]
    fz = hf_ref[...]
    f = lb + (1.0 - lb) * jax.nn.sigmoid(fz)
    logf = jnp.log(jnp.maximum(f, 1e-30))
    kk = (1.0 - lb) * jax.nn.sigmoid(-fz)
    qz = hq_ref[...]
    qq = qz * jax.nn.sigmoid(qz)
    rowc = lax.broadcasted_iota(I32, (tt, 1), 0) % chunk
    b = logf
    d = 1
    while d < chunk:
        b = b + jnp.where(rowc >= d, pltpu.roll(b, d, 0), 0.0)
        d *= 2
    b3 = b.reshape(nblk, chunk, C_HEADS * C_DK)
    blast3 = b3[:, chunk - 1:chunk, :]
    bmid3 = b3[:, chunk // 2:chunk // 2 + 1, :]
    blast = jnp.broadcast_to(blast3, b3.shape).reshape(tt, C_HEADS * C_DK)
    bmid = jnp.broadcast_to(bmid3, b3.shape).reshape(tt, C_HEADS * C_DK)
    b_ref[...] = b
    qi_ref[...] = qq * jnp.exp(b)
    qa_ref[...] = qq * jnp.exp(jnp.clip(b - bmid, -EXP_CLAMP, EXP_CLAMP))
    ka_ref[...] = kk * jnp.exp(jnp.clip(bmid - b, -EXP_CLAMP, EXP_CLAMP))
    ks_ref[...] = kk * jnp.exp(blast - b)

    bdk = bdk_ref[...]
    bds = bds_ref[...]
    cmask = cm_ref[...]

    def tile_rows(x):
        return jnp.broadcast_to(x[None], (C_HEADS,) + x.shape).reshape(C_HEADS * chunk, x.shape[1])

    def block_body(jb, carry):
        r0 = pl.multiple_of(jb * chunk, chunk)
        rows = pl.ds(r0, chunk)
        st = st_ref[...]
        v = hi_ref[rows, :]
        o_inter = _nt_dot(qi_ref[rows, :].astype(BF16), st.astype(BF16))
        kabd = (tile_rows(ka_ref[rows, :]) * bdk).astype(BF16)
        amat = _nt_dot(qa_ref[rows, :].astype(BF16), kabd) * cmask
        vbd = (tile_rows(v) * bdk).astype(BF16)
        o_intra = jnp.dot(amat.astype(BF16), vbd, preferred_element_type=F32)
        oo_ref[rows, :] = o_inter + o_intra
        dst = _tn_dot(v.astype(BF16), ks_ref[rows, :].astype(BF16))
        el = jnp.exp(b_ref[pl.ds(r0 + chunk - 1, 1), :])
        st_ref[...] = st * el + dst * bds
        return carry

    lax.fori_loop(0, nblk, block_body, 0)

    o = oo_ref[...]
    sq = o * o
    hi = sq.astype(BF16)
    lo = (sq - hi.astype(F32)).astype(BF16)
    seg = seg_ref[...]
    ms = jnp.dot(hi, seg, preferred_element_type=F32) + jnp.dot(lo, seg, preferred_element_type=F32)
    gz = hg_ref[...]
    o_ref[:, B_WIDTH:] = o * lax.rsqrt(ms + EPS) * ng_ref[...] * (gz * jax.nn.sigmoid(gz))

    @pl.when(t == pl.num_programs(1) - 1)
    def _():
        stt = st_ref[...].T
        for h in range(C_HEADS):
            so_ref[0, h] = stt[h * C_DK:(h + 1) * C_DK, h * C_DV:(h + 1) * C_DV]


def _mix(p, row0, nseq, seq, tt, chunk, conv_state, s0, cw, lb, ng, consts):
    bdk, bds, cmask, seg = consts
    nt = seq // tt
    rb = row0 // tt

    def pspec(col):
        return pl.BlockSpec((tt, B_WIDTH), lambda s, t: (rb + s * nt + t, col // B_WIDTH))

    def const_spec(a):
        return pl.BlockSpec(a.shape, lambda s, t: (0,) * a.ndim)

    w = C_HEADS * C_DK
    lb = lb.reshape(1, w)
    ng = ng.reshape(1, C_WIDTH)
    return pl.pallas_call(
        functools.partial(_mix_kernel, chunk=chunk),
        grid=(nseq, nt),
        in_specs=[pspec(c) for c in (COL_BG, COL_CG, COL_BH, COL_HF, COL_HI, COL_HQ, COL_HG)] + [
            pl.BlockSpec((1, CONV_K - 1, B_WIDTH), lambda s, t: (s, 0, 0)),
            pl.BlockSpec((1, C_HEADS, C_DK, C_DV), lambda s, t: (s, 0, 0, 0)),
            const_spec(cw), const_spec(lb), const_spec(ng),
            const_spec(bdk), const_spec(bds), const_spec(cmask), const_spec(seg),
        ],
        out_specs=[
            pl.BlockSpec((tt, B_WIDTH + C_WIDTH), lambda s, t: (s * nt + t, 0)),
            pl.BlockSpec((1, CONV_K - 1, B_WIDTH), lambda s, t: (s, 0, 0)),
            pl.BlockSpec((1, C_HEADS, C_DK, C_DV), lambda s, t: (s, 0, 0, 0)),
        ],
        out_shape=[
            jax.ShapeDtypeStruct((nseq * seq, B_WIDTH + C_WIDTH), F32),
            jax.ShapeDtypeStruct((nseq, CONV_K - 1, B_WIDTH), F32),
            jax.ShapeDtypeStruct((nseq, C_HEADS, C_DK, C_DV), F32),
        ],
        scratch_shapes=[
            pltpu.VMEM((tt + SUBLANES, B_WIDTH), F32),
            pltpu.VMEM((C_WIDTH, w), F32),
            pltpu.VMEM((tt, w), F32),
            pltpu.VMEM((tt, w), F32),
            pltpu.VMEM((tt, w), F32),
            pltpu.VMEM((tt, w), F32),
            pltpu.VMEM((tt, w), F32),
            pltpu.VMEM((tt, C_WIDTH), F32),
        ],
        compiler_params=_cparams(("arbitrary", "arbitrary")),
        name="mix_c%d" % chunk,
    )(p, p, p, p, p, p, p, conv_state, s0, cw, lb, ng, bdk, bds, cmask, seg)


def _mix_consts(chunk):
    w = C_HEADS * C_DK
    rh = jnp.arange(C_HEADS * chunk)[:, None] // chunk
    ch = jnp.arange(w)[None, :] // C_DK
    bdk = (rh == ch).astype(F32)
    hv = jnp.arange(C_WIDTH)[:, None] // C_DV
    bds = (hv == ch).astype(F32)
    tq = jnp.arange(chunk)[:, None]
    sk = jnp.arange(C_HEADS * chunk)[None, :] % chunk
    cmask = (sk <= tq).astype(F32)
    seg = (bds / C_DV).astype(BF16)
    return bdk, bds, cmask, seg


def _t5_bucket(dist):
    dist = jnp.maximum(dist, 0)
    max_exact = N_BUCKETS // 2
    d = jnp.maximum(dist, 1).astype(F32)
    large = max_exact + (jnp.log(d / max_exact) / math.log(MAX_DISTANCE / max_exact)
                         * (N_BUCKETS - max_exact)).astype(I32)
    large = jnp.clip(large, 0, N_BUCKETS - 1)
    return jnp.where(dist < max_exact, dist, large)


def _bias_of_dist(rel_bias, dist):
    bias = jnp.moveaxis(rel_bias.astype(F32)[_t5_bucket(dist)], -1, 0) * LOG2E
    return jnp.where((dist >= 0)[None], bias, NEG_BIG)


def _far_bias(rel_bias, min_dist):
    assert int(16 + math.log(min_dist / 16) / math.log(MAX_DISTANCE / 16) * 16) >= N_BUCKETS - 1
    return rel_bias.astype(F32)[N_BUCKETS - 1] * LOG2E


def _permute_w_in(w_in):
    sizes = (A_WIDTH, A_WIDTH, A_WIDTH, IDX_HEADS * IDX_DIM, IDX_DIM, IDX_HEADS,
             B_WIDTH, B_WIDTH, B_WIDTH, C_HEADS * C_DK, C_WIDTH, C_HEADS * C_DK, C_WIDTH)
    offs = [0]
    for s in sizes:
        offs.append(offs[-1] + s)
    cols = lambda i: w_in[..., offs[i]:offs[i + 1]]
    padw = LANES - IDX_DIM - IDX_HEADS
    pad = jnp.zeros(w_in.shape[:-1] + (padw,), w_in.dtype)
    parts = [cols(i) for i in (0, 1, 2, 3, 6, 7, 8, 9, 10, 11, 12)] + [cols(4), cols(5), pad]
    return jnp.concatenate(parts, axis=-1).astype(BF16)


def kernel(x_prompt, x_sample, cache_k, cache_v, cache_kidx, state_conv, state_hgrn, page_table,
           w_in, w_out, conv_w, hgrn_lb_logits, hgrn_norm_g, rel_bias, norm1_g, norm2_g,
           w_ff1, w_ff2, final_g):
    batch, seq, d_model = x_prompt.shape
    dec_batch, dec_seq, _ = x_sample.shape
    depth = w_in.shape[0]
    n_pool, page = cache_k.shape[1], cache_k.shape[2]
    n_pages = page_table.shape[1]
    past = n_pages * page
    n_p, n_s = batch * seq, dec_batch * dec_seq
    n = n_p + n_s
    assert dec_seq == SUBLANES and seq % (2 * TQ) == 0 and seq % SCORE_CHUNK == 0
    tm = 512 if n % 512 == 0 else 256
    assert n % tm == 0 and n_p % tm == 0

    x = jnp.concatenate([x_prompt.reshape(n_p, d_model), x_sample.reshape(n_s, d_model)], axis=0)
    w_in_p = _permute_w_in(w_in)
    w_out_b = w_out.astype(BF16)
    w_ff1_b = w_ff1.astype(BF16)
    w_ff2_b = w_ff2.astype(BF16)
    sm = jax.nn.softmax(hgrn_lb_logits.astype(F32), axis=0)
    lb_all = jnp.cumsum(sm, axis=0) - sm[0]

    ck = jnp.transpose(cache_k, (0, 1, 3, 4, 2))
    cv = jnp.transpose(cache_v, (0, 1, 3, 4, 2))
    cki = jnp.transpose(cache_kidx, (0, 1, 3, 2))

    qi_ = jnp.arange(TQ)[None, :]
    kj_ = jnp.arange(2 * TQ)[:, None]
    tab_p = jnp.stack([_bias_of_dist(rel_bias, qi_ - kj_), _bias_of_dist(rel_bias, TQ + qi_ - kj_)])
    cfar_p = _far_bias(rel_bias, TQ + 1)
    rows = A_HEADS * dec_seq
    si = jnp.arange(dec_seq)[:, None]
    sj = jnp.arange(LANES)[None, :]
    tab_last = _bias_of_dist(rel_bias, page + si - sj).reshape(rows, LANES)
    d_new = jnp.where(sj < dec_seq, si - sj, -1)
    tab_new = _bias_of_dist(rel_bias, d_new).reshape(rows, LANES)
    cfar_s = jnp.broadcast_to(jnp.repeat(_far_bias(rel_bias, page + 1), dec_seq)[:, None], (rows, LANES))
    bd_s = (jnp.arange(rows)[:, None] // dec_seq == jnp.arange(A_WIDTH)[None, :] // HEAD_DIM).astype(F32)
    tabs_s = (tab_last, tab_new, cfar_s, bd_s)

    chunk_p = math.gcd(seq, HGRN_CHUNK)
    chunk_s = math.gcd(dec_seq, HGRN_CHUNK)
    tt_p = 512 if seq % 512 == 0 else seq
    consts_p = _mix_consts(chunk_p)
    consts_s = _mix_consts(chunk_s)
    conv0 = jnp.zeros((batch, CONV_K - 1, B_WIDTH), F32)
    hgrn0 = jnp.zeros((batch, C_HEADS, C_DK, C_DV), F32)

    ks, vs, kis, cps, sps, css, sss = [], [], [], [], [], [], []
    for l in range(depth):
        p, qt, qit, kb, vt, ikwb, iwt = _inproj(x, norm1_g[l], w_in_p[l], tm)
        a_p = _dsa_prompt(iwt, qt, qit, kb, ikwb, vt, tab_p, cfar_p, batch, seq)
        a_s = _dsa_sample(p, n_p, cki, ck, cv, l, page_table, tabs_s, dec_batch, dec_seq)
        bc_p, conv_p, s_p = _mix(p, 0, batch, seq, tt_p, chunk_p, conv0, hgrn0,
                                 conv_w[l], lb_all[l], hgrn_norm_g[l], consts_p)
        bc_s, conv_s, s_s = _mix(p, n_p, dec_batch, dec_seq, dec_seq, chunk_s, state_conv[l],
                                 state_hgrn[l], conv_w[l], lb_all[l], hgrn_norm_g[l], consts_s)
        a = jnp.concatenate([a_p, a_s], axis=0)
        bc = jnp.concatenate([bc_p, bc_s], axis=0)
        x = _outffn(x, a, bc, w_out_b[l], norm2_g[l], w_ff1_b[l], w_ff2_b[l], final_g,
                    tm, 1024, l == depth - 1)
        ks.append(p[:, COL_K:COL_K + A_WIDTH])
        vs.append(p[:, COL_V:COL_V + A_WIDTH])
        kis.append(p[:, COL_IKW:COL_IKW + IDX_DIM])
        cps.append(conv_p); sps.append(s_p); css.append(conv_s); sss.append(s_s)

    kall, vall, kiall = jnp.stack(ks), jnp.stack(vs), jnp.stack(kis)
    hd = lambda a, lo, hi, shp: a[:, lo:hi].reshape((depth,) + shp)
    return (
        x[:n_p].reshape(batch, seq, d_model),
        x[n_p:].reshape(dec_batch, dec_seq, d_model),
        hd(kall, 0, n_p, (batch, seq, A_HEADS, HEAD_DIM)),
        hd(vall, 0, n_p, (batch, seq, A_HEADS, HEAD_DIM)),
        hd(kiall, 0, n_p, (batch, seq, IDX_DIM)),
        jnp.stack(cps), jnp.stack(sps),
        hd(kall, n_p, n, (dec_batch, dec_seq, A_HEADS, HEAD_DIM)),
        hd(vall, n_p, n, (dec_batch, dec_seq, A_HEADS, HEAD_DIM)),
        hd(kiall, n_p, n, (dec_batch, dec_seq, IDX_DIM)),
        jnp.stack(css), jnp.stack(sss),
    )
```

```python
import functools
import math

import jax
import jax.numpy as jnp
from jax import lax
from jax.experimental import pallas as pl
from jax.experimental.pallas import tpu as pltpu

F32 = jnp.float32
BF16 = jnp.bfloat16
I32 = jnp.int32

HEAD_DIM = 64
A_HEADS = 8
A_WIDTH = A_HEADS * HEAD_DIM
IDX_HEADS = 8
IDX_DIM = 64
B_WIDTH = 256
C_HEADS = 4
C_DK = 64
C_DV = 64
C_WIDTH = C_HEADS * C_DV
TOPK_MAX = 256
CONV_K = 3
N_BUCKETS = 32
MAX_DISTANCE = 128
HGRN_CHUNK = 16
HGRN_UNROLL = 4
MIX_SEQS_PER_STEP = 8
SAMPLE_SEQS_PER_STEP = 2
EPS = 1e-6
NEG_BIG = -1e30
LOG2E = 1.4426950408889634
INT_MIN = -(2 ** 31)
INT_MAX = 2 ** 31 - 1
EXP_CLAMP = 80.0

LANES = 128
SUBLANES = 8
VMEM_LIMIT = 56 * 1024 * 1024

COL_Q, COL_K, COL_V, COL_QI = 0, 512, 1024, 1536
COL_BG, COL_CG, COL_BH, COL_HF, COL_HI, COL_HQ, COL_HG = (2048 + 256 * i for i in range(7))
COL_IKW = 3840
P_WIDTH = 3968

TQ = 128
SCORE_CHUNK = 512
ATT_CHUNK = 512


def _cparams(sem):
    return pltpu.CompilerParams(dimension_semantics=sem, vmem_limit_bytes=VMEM_LIMIT)


def _nt_dot(a, b):
    return lax.dot_general(a, b, (((1,), (1,)), ((), ())), preferred_element_type=F32)


def _tn_dot(a, b):
    return lax.dot_general(a, b, (((0,), (0,)), ((), ())), preferred_element_type=F32)


def _float_key(x):
    bits = pltpu.bitcast(x, I32)
    return bits ^ ((bits >> 31) & INT_MAX)


def _inproj_kernel(x_ref, g_ref, w_ref, ko_in, vo_in, io_in,
                   p_ref, qt_ref, qit_ref, kb_ref, vt_ref, ikw_ref, iwt_ref, ko_ref, vo_ref, io_ref,
                   *, n_prompt_blocks):
    del ko_in, vo_in, io_in
    x = x_ref[...]
    ms = jnp.mean(x * x, axis=-1, keepdims=True)
    xn = (x * lax.rsqrt(ms + EPS) * g_ref[...]).astype(BF16)
    res = jnp.dot(xn, w_ref[...], preferred_element_type=F32)
    p_ref[...] = res
    q_scale = (HEAD_DIM ** -0.5) * LOG2E
    w_scale = (IDX_HEADS ** -0.5) * (IDX_DIM ** -0.5)
    qt_ref[...] = (res[:, COL_Q:COL_Q + A_WIDTH] * q_scale).T.astype(BF16)
    qit_ref[...] = res[:, COL_QI:COL_QI + A_WIDTH].T.astype(BF16)
    k = res[:, COL_K:COL_K + A_WIDTH]
    kb_ref[...] = k.astype(BF16)
    vt = res[:, COL_V:COL_V + A_WIDTH].T
    vt_ref[...] = vt.astype(BF16)
    ikw = res[:, COL_IKW:COL_IKW + LANES]
    ikw_ref[...] = ikw.astype(BF16)
    ikwt = ikw.T
    iwt_ref[...] = ikwt[IDX_DIM:IDX_DIM + IDX_HEADS] * w_scale

    @pl.when(pl.program_id(0) < n_prompt_blocks)
    def _():
        ko_ref[...] = k.T
        vo_ref[...] = vt
        io_ref[...] = ikwt[:IDX_DIM]


def _inproj(x, g, w, tm, layer, kbuf, vbuf, ibuf):
    n, d = x.shape
    batch, seq = kbuf.shape[1], kbuf.shape[3]
    tpb = seq // tm
    npb = batch * tpb

    def slab_spec(width):
        def index(i):
            ii = jnp.minimum(i, npb - 1)
            return (layer, ii // tpb, 0, ii % tpb)
        return pl.BlockSpec((None, None, width, tm), index)

    any_spec = pl.BlockSpec(memory_space=pl.ANY)
    return pl.pallas_call(
        functools.partial(_inproj_kernel, n_prompt_blocks=npb),
        grid=(n // tm,),
        in_specs=[
            pl.BlockSpec((tm, d), lambda i: (i, 0)),
            pl.BlockSpec((1, d), lambda i: (0, 0)),
            pl.BlockSpec((d, P_WIDTH), lambda i: (0, 0)),
            any_spec, any_spec, any_spec,
        ],
        input_output_aliases={3: 7, 4: 8, 5: 9},
        out_specs=[
            pl.BlockSpec((tm, P_WIDTH), lambda i: (i, 0)),
            pl.BlockSpec((A_WIDTH, tm), lambda i: (0, i)),
            pl.BlockSpec((A_WIDTH, tm), lambda i: (0, i)),
            pl.BlockSpec((tm, A_WIDTH), lambda i: (i, 0)),
            pl.BlockSpec((A_WIDTH, tm), lambda i: (0, i)),
            pl.BlockSpec((tm, LANES), lambda i: (i, 0)),
            pl.BlockSpec((IDX_HEADS, tm), lambda i: (0, i)),
            slab_spec(A_WIDTH), slab_spec(A_WIDTH), slab_spec(IDX_DIM),
        ],
        out_shape=[
            jax.ShapeDtypeStruct((n, P_WIDTH), F32),
            jax.ShapeDtypeStruct((A_WIDTH, n), BF16),
            jax.ShapeDtypeStruct((A_WIDTH, n), BF16),
            jax.ShapeDtypeStruct((n, A_WIDTH), BF16),
            jax.ShapeDtypeStruct((A_WIDTH, n), BF16),
            jax.ShapeDtypeStruct((n, LANES), BF16),
            jax.ShapeDtypeStruct((IDX_HEADS, n), F32),
            jax.ShapeDtypeStruct(kbuf.shape, F32),
            jax.ShapeDtypeStruct(vbuf.shape, F32),
            jax.ShapeDtypeStruct(ibuf.shape, F32),
        ],
        compiler_params=_cparams(("arbitrary",)),
        name="inproj",
    )(x, g.reshape(1, d), w, kbuf, vbuf, ibuf)


def _outffn_kernel(x_ref, ap_ref, bcp_ref, as_ref, bcs_ref, wo_ref, g2_ref, w1_ref, w2_ref, gf_ref,
                   o_ref, acc_ref, hn_ref, *, final, n_prompt_blocks):
    i = pl.program_id(0)
    j = pl.program_id(1)

    def mix_in(a_ref, bc_ref):
        x1 = x_ref[...]
        x1 = x1 + jnp.dot(a_ref[...].astype(BF16), wo_ref[:A_WIDTH, :], preferred_element_type=F32)
        x1 = x1 + jnp.dot(bc_ref[...].astype(BF16), wo_ref[A_WIDTH:, :], preferred_element_type=F32)
        acc_ref[...] = x1
        ms = jnp.mean(x1 * x1, axis=-1, keepdims=True)
        hn_ref[...] = (x1 * lax.rsqrt(ms + EPS) * g2_ref[...]).astype(BF16)

    @pl.when(jnp.logical_and(j == 0, i < n_prompt_blocks))
    def _():
        mix_in(ap_ref, bcp_ref)

    @pl.when(jnp.logical_and(j == 0, i >= n_prompt_blocks))
    def _():
        mix_in(as_ref, bcs_ref)

    h = jnp.dot(hn_ref[...], w1_ref[...], preferred_element_type=F32)
    h = jnp.maximum(h, 0.0)
    h = (h * h).astype(BF16)
    acc_ref[...] += jnp.dot(h, w2_ref[...], preferred_element_type=F32)

    @pl.when(j == pl.num_programs(1) - 1)
    def _():
        y = acc_ref[...]
        if final:
            ms = jnp.mean(y * y, axis=-1, keepdims=True)
            y = y * lax.rsqrt(ms + EPS) * gf_ref[...]
        o_ref[...] = y


def _outffn(x, a_p, bc_p, a_s, bc_s, wo, g2, w1, w2, gf, tm, tf, final):
    n, d = x.shape
    dff = w1.shape[1]
    npb = a_p.shape[0] // tm
    prompt_rows = lambda i, j: (jnp.minimum(i, npb - 1), 0)
    sample_rows = lambda i, j: (jnp.maximum(i - npb, 0), 0)
    return pl.pallas_call(
        functools.partial(_outffn_kernel, final=final, n_prompt_blocks=npb),
        grid=(n // tm, dff // tf),
        in_specs=[
            pl.BlockSpec((tm, d), lambda i, j: (i, 0)),
            pl.BlockSpec((tm, A_WIDTH), prompt_rows),
            pl.BlockSpec((tm, B_WIDTH + C_WIDTH), prompt_rows),
            pl.BlockSpec((tm, A_WIDTH), sample_rows),
            pl.BlockSpec((tm, B_WIDTH + C_WIDTH), sample_rows),
            pl.BlockSpec(wo.shape, lambda i, j: (0, 0)),
            pl.BlockSpec((1, d), lambda i, j: (0, 0)),
            pl.BlockSpec((d, tf), lambda i, j: (0, j)),
            pl.BlockSpec((tf, d), lambda i, j: (j, 0)),
            pl.BlockSpec((1, d), lambda i, j: (0, 0)),
        ],
        out_specs=pl.BlockSpec((tm, d), lambda i, j: (i, 0)),
        out_shape=jax.ShapeDtypeStruct((n, d), F32),
        scratch_shapes=[pltpu.VMEM((tm, d), F32), pltpu.VMEM((tm, d), BF16)],
        compiler_params=_cparams(("arbitrary", "arbitrary")),
        name="outffn_final" if final else "outffn",
    )(x, a_p, bc_p, a_s, bc_s, wo, g2.reshape(1, d), w1, w2, gf.reshape(1, d))


def _topk_threshold(count_ge, count_eq_before, like, topk, n_pos):
    kf = float(topk)
    ans = _bisect_keys(count_ge, like, topk)
    c_ans = count_ge(ans)
    pos_bits = max(1, (n_pos - 1).bit_length())

    def tie_cut(_):
        need = kf - count_ge(ans + 1)

        def jbody(i, xcut):
            cand = xcut + (jnp.int32(1) << (pos_bits - 1 - i))
            return jnp.where(count_eq_before(ans, cand) < need, cand, xcut)

        cut = lax.fori_loop(0, pos_bits, jbody, jnp.zeros_like(like))
        return jnp.where(c_ans > kf, cut, n_pos)

    jcut = lax.cond(jnp.max(c_ans) > kf, tie_cut, lambda _: jnp.full_like(like, n_pos), 0)
    return ans, jcut


def _bisect_keys(count_ge, like, topk):
    kf = float(topk)
    ans = jnp.where(count_ge(jnp.zeros_like(like)) >= kf, 0, INT_MIN).astype(I32)

    def body(i, a):
        cand = a + (jnp.int32(1) << (30 - i))
        return jnp.where(count_ge(cand) >= kf, cand, a)

    return lax.fori_loop(0, 31, body, ans)


def _dsa_prompt_kernel(cfar_ref, iwt_ref, qt_ref, qit_ref, kb_ref, ikw_ref, vt_ref, tab_ref, o_ref,
                       sc_ref, qbd_ref, qibd_ref, acc0_ref, acc1_ref, acc2_ref, acc3_ref,
                       ans_ref, jcut_ref, *, seq, topk):
    qb = pl.program_id(1)
    tq = TQ
    qs = qb * tq
    npair = A_HEADS // 2

    zq = jnp.zeros((HEAD_DIM, tq), BF16)
    for j in range(npair):
        top = qt_ref[(2 * j) * HEAD_DIM:(2 * j + 1) * HEAD_DIM, :]
        bot = qt_ref[(2 * j + 1) * HEAD_DIM:(2 * j + 2) * HEAD_DIM, :]
        qbd_ref[j] = jnp.concatenate([jnp.concatenate([top, zq], axis=1),
                                      jnp.concatenate([zq, bot], axis=1)], axis=0)
        itop = qit_ref[(2 * j) * IDX_DIM:(2 * j + 1) * IDX_DIM, :]
        ibot = qit_ref[(2 * j + 1) * IDX_DIM:(2 * j + 2) * IDX_DIM, :]
        qibd_ref[j] = jnp.concatenate([jnp.concatenate([itop, ibot], axis=1),
                                       jnp.zeros((LANES - IDX_DIM, 2 * tq), BF16)], axis=0)

    wrow = iwt_ref[...]
    tpos = qs + lax.broadcasted_iota(I32, (1, tq), 1)

    def fold(x, op):
        while x.shape[0] > SUBLANES:
            half = x.shape[0] // 2
            x = op(x[:half], x[half:])
        return x

    n_sc = (qs + tq + SCORE_CHUNK - 1) // SCORE_CHUNK

    def score_body(c, carry):
        c0 = pl.multiple_of(c * SCORE_CHUNK, SCORE_CHUNK)
        ikc = ikw_ref[pl.ds(c0, SCORE_CHUNK), :]
        score = jnp.zeros((SCORE_CHUNK, tq), F32)
        for j in range(npair):
            s = jnp.dot(ikc, qibd_ref[j], preferred_element_type=F32)
            score = score + jnp.maximum(s[:, :tq], 0.0) * wrow[2 * j:2 * j + 1, :]
            score = score + jnp.maximum(s[:, tq:], 0.0) * wrow[2 * j + 1:2 * j + 2, :]
        pos = c0 + lax.broadcasted_iota(I32, (SCORE_CHUNK, tq), 0)
        sc_ref[pl.ds(c0, SCORE_CHUNK), :] = _float_key(jnp.where(pos <= tpos, score, NEG_BIG))
        return carry

    lax.fori_loop(0, n_sc, score_body, 0)

    def count_ge(cand):
        def body(c, acc):
            c0 = pl.multiple_of(c * SCORE_CHUNK, SCORE_CHUNK)
            keys = sc_ref[pl.ds(c0, SCORE_CHUNK), :]
            return acc + fold(jnp.where(keys >= cand, 1.0, 0.0), jnp.add)
        acc = lax.fori_loop(0, n_sc, body, jnp.zeros((SUBLANES, tq), F32))
        return jnp.sum(acc, axis=0, keepdims=True)

    def count_eq_before(a, xcut):
        def body(c, acc):
            c0 = pl.multiple_of(c * SCORE_CHUNK, SCORE_CHUNK)
            keys = sc_ref[pl.ds(c0, SCORE_CHUNK), :]
            pos = c0 + lax.broadcasted_iota(I32, (SCORE_CHUNK, tq), 0)
            hit = jnp.where(keys == a, jnp.where(pos < xcut, 1.0, 0.0), 0.0)
            return acc + fold(hit, jnp.add)
        acc = lax.fori_loop(0, n_sc, body, jnp.zeros((SUBLANES, tq), F32))
        return jnp.sum(acc, axis=0, keepdims=True)

    ans_ref[...] = jnp.full((1, tq), INT_MIN + 1, I32)
    jcut_ref[...] = jnp.full((1, tq), seq, I32)

    @pl.when(qs + tq > topk)
    def _():
        ans, jcut = _topk_threshold(count_ge, count_eq_before, jnp.zeros((1, tq), I32), topk, seq)
        ans_ref[...] = ans
        jcut_ref[...] = jcut

    acc_refs = (acc0_ref, acc1_ref, acc2_ref, acc3_ref)
    for r in acc_refs:
        r[...] = jnp.zeros(r.shape, F32)
    ans = ans_ref[...]
    jcut = jcut_ref[...]

    def attend_chunk(c0, width, limit, tsel, ms, lss):
        keys = sc_ref[pl.ds(c0, width), :]
        pos = c0 + lax.broadcasted_iota(I32, (width, tq), 0)
        thr = jnp.where(pos <= jcut, ans - 1, ans)
        thr = jnp.where(pos < limit, thr, INT_MAX)
        sel = keys > thr
        lgs = [jnp.dot(kb_ref[pl.ds(c0, width), j * LANES:(j + 1) * LANES], qbd_ref[j],
                       preferred_element_type=F32) for j in range(npair)]
        ms_new, lss_new = [], []
        for j in range(npair):
            ps, alphas = [], []
            for half in range(2):
                h = 2 * j + half
                lh = lgs[j][:, half * tq:(half + 1) * tq]
                if tsel is None:
                    cb = cfar_ref[h]
                else:
                    lh = lh + tab_ref[tsel, h]
                    cb = 0.0
                lh = jnp.where(sel, lh, NEG_BIG)
                cmax = jnp.max(fold(lh, jnp.maximum), axis=0, keepdims=True)
                m_new = jnp.maximum(ms[h], cmax + cb)
                alpha = jnp.exp2(ms[h] - m_new)
                p = jnp.exp2(lh - (m_new - cb))
                ms_new.append(m_new)
                lss_new.append(alpha * lss[h] + fold(p, jnp.add))
                ps.append(p.astype(BF16))
                alphas.append(alpha)
            vt = vt_ref[j * LANES:(j + 1) * LANES, pl.ds(c0, width)]
            pv = jnp.dot(vt, jnp.concatenate(ps, axis=1), preferred_element_type=F32)
            acc_refs[j][...] = acc_refs[j][...] * jnp.concatenate(alphas, axis=1) + pv
        return tuple(ms_new), tuple(lss_new)

    near0 = jnp.maximum(qb - 1, 0) * tq

    def far_body(c, carry):
        return attend_chunk(pl.multiple_of(c * ATT_CHUNK, ATT_CHUNK), ATT_CHUNK, near0, None, *carry)

    init = (tuple(jnp.full((1, tq), NEG_BIG, F32) for _ in range(A_HEADS)),
            tuple(jnp.zeros((SUBLANES, tq), F32) for _ in range(A_HEADS)))
    carry = lax.fori_loop(0, (near0 + ATT_CHUNK - 1) // ATT_CHUNK, far_body, init)
    _, lss = attend_chunk(pl.multiple_of(near0, tq), 2 * tq, seq, jnp.minimum(qb, 1), *carry)

    for j in range(npair):
        outs = []
        for half in range(2):
            inv = 1.0 / jnp.sum(lss[2 * j + half], axis=0, keepdims=True)
            blk = acc_refs[j][half * HEAD_DIM:(half + 1) * HEAD_DIM, half * tq:(half + 1) * tq]
            outs.append((blk * inv).T)
        o_ref[:, j * LANES:(j + 1) * LANES] = jnp.concatenate(outs, axis=1)


def _dsa_prompt(iwt, qt, qit, kb, ikwb, vt, tab, cfar, batch, seq):
    tq = TQ
    nq = seq // tq
    topk = min(TOPK_MAX, seq // 4)
    kern = functools.partial(_dsa_prompt_kernel, seq=seq, topk=topk)
    return pl.pallas_call(
        kern,
        grid=(batch, nq),
        in_specs=[
            pl.BlockSpec(memory_space=pltpu.SMEM),
            pl.BlockSpec((IDX_HEADS, tq), lambda b, i: (0, b * nq + i)),
            pl.BlockSpec((A_WIDTH, tq), lambda b, i: (0, b * nq + i)),
            pl.BlockSpec((A_WIDTH, tq), lambda b, i: (0, b * nq + i)),
            pl.BlockSpec((seq, A_WIDTH), lambda b, i: (b, 0)),
            pl.BlockSpec((seq, LANES), lambda b, i: (b, 0)),
            pl.BlockSpec((A_WIDTH, seq), lambda b, i: (0, b)),
            pl.BlockSpec(tab.shape, lambda b, i: (0, 0, 0, 0)),
        ],
        out_specs=pl.BlockSpec((tq, A_WIDTH), lambda b, i: (b * nq + i, 0)),
        out_shape=jax.ShapeDtypeStruct((batch * seq, A_WIDTH), F32),
        scratch_shapes=[
            pltpu.VMEM((seq, tq), I32),
            pltpu.VMEM((A_HEADS // 2, 2 * HEAD_DIM, 2 * tq), BF16),
            pltpu.VMEM((IDX_HEADS // 2, LANES, 2 * tq), BF16),
            pltpu.VMEM((2 * HEAD_DIM, 2 * tq), F32),
            pltpu.VMEM((2 * HEAD_DIM, 2 * tq), F32),
            pltpu.VMEM((2 * HEAD_DIM, 2 * tq), F32),
            pltpu.VMEM((2 * HEAD_DIM, 2 * tq), F32),
            pltpu.VMEM((1, tq), I32),
            pltpu.VMEM((1, tq), I32),
        ],
        compiler_params=_cparams(("arbitrary", "arbitrary")),
        name="dsa_prompt",
    )(cfar, iwt, qt, qit, kb, ikwb, vt, tab)


def _dsa_sample_kernel(pt_ref, q_ref, kn_ref, vn_ref, qi_ref, ikw_ref, *rest, n_pages, page, topk, nb):
    npg = nb * n_pages
    kidx_refs, k_refs, v_refs = rest[:npg], rest[npg:2 * npg], rest[2 * npg:3 * npg]
    tab_last_ref, tab_new_ref, cfar_ref, bd_ref, o_ref = rest[3 * npg:]
    ds = q_ref.shape[0] // nb
    rows = A_HEADS * ds
    past = n_pages * page
    width = past + LANES
    w_scale = (IDX_HEADS ** -0.5) * (IDX_DIM ** -0.5)
    q_scale = (HEAD_DIM ** -0.5) * LOG2E

    def tile_heads(x):
        return jnp.broadcast_to(x[None], (A_HEADS,) + x.shape).reshape(rows, x.shape[1])

    def pad_rows(x):
        return jnp.concatenate([x, jnp.zeros((LANES - ds, x.shape[1]), x.dtype)], axis=0)

    qrow = lax.broadcasted_iota(I32, (ds, LANES), 0)
    kcol = lax.broadcasted_iota(I32, (ds, LANES), 1)

    def seq_scores(s):
        r = slice(s * ds, (s + 1) * ds)
        qi = qi_ref[r, :]
        q2 = jnp.concatenate([qi[:, h * IDX_DIM:(h + 1) * IDX_DIM] for h in range(IDX_HEADS)],
                             axis=0).astype(BF16)
        wi = ikw_ref[r, IDX_DIM:IDX_DIM + IDX_HEADS] * w_scale
        wcols = [jnp.broadcast_to(wi[:, h:h + 1], (ds, LANES)) for h in range(IDX_HEADS)]

        def head_sum(sc):
            out = jnp.zeros((ds, LANES), F32)
            for h in range(IDX_HEADS):
                out = out + jnp.maximum(sc[h * ds:(h + 1) * ds], 0.0) * wcols[h]
            return out

        scores = [head_sum(jnp.dot(q2, kidx_refs[s * n_pages + p][0].astype(BF16),
                                   preferred_element_type=F32)) for p in range(n_pages)]
        ik_new = pad_rows(ikw_ref[r, :IDX_DIM]).astype(BF16)
        s_new = jnp.where(kcol <= qrow, head_sum(_nt_dot(q2, ik_new)), NEG_BIG)
        return jnp.concatenate(scores + [s_new], axis=1)

    keys = _float_key(jnp.concatenate([seq_scores(s) for s in range(nb)], axis=0))
    pos = lax.broadcasted_iota(I32, (1, width), 1)

    def count_ge(cand):
        return jnp.sum(jnp.where(keys >= cand, 1.0, 0.0), axis=1, keepdims=True)

    def count_eq_before(a, xcut):
        hit = jnp.where(keys == a, jnp.where(pos < xcut, 1.0, 0.0), 0.0)
        return jnp.sum(hit, axis=1, keepdims=True)

    ans, jcut = _topk_threshold(count_ge, count_eq_before, jnp.zeros((nb * ds, 1), I32), topk, width)
    thr = jnp.where(pos <= jcut, ans - 1, ans)
    addmask_all = jnp.where(keys > thr, 0.0, NEG_BIG)

    bd = bd_ref[...]
    for s in range(nb):
        r = slice(s * ds, (s + 1) * ds)
        addmask = addmask_all[r]
        qbd = (tile_heads(q_ref[r, :] * q_scale) * bd).astype(BF16)
        logits = []
        for p in range(n_pages):
            kt = k_refs[s * n_pages + p][0].reshape(A_WIDTH, page).astype(BF16)
            lg = jnp.dot(qbd, kt, preferred_element_type=F32)
            lg = lg + (tab_last_ref[...] if p == n_pages - 1 else cfar_ref[...])
            logits.append(lg + tile_heads(addmask[:, p * page:(p + 1) * page]))
        lg = _nt_dot(qbd, pad_rows(kn_ref[r, :]).astype(BF16)) + tab_new_ref[...]
        logits.append(lg + tile_heads(addmask[:, past:]))
        m = functools.reduce(jnp.maximum, [jnp.max(l, axis=1, keepdims=True) for l in logits])
        acc = jnp.zeros((rows, A_WIDTH), F32)
        lsum = jnp.zeros((rows, 1), F32)
        for p in range(n_pages + 1):
            pr = jnp.exp2(logits[p] - m)
            lsum = lsum + jnp.sum(pr, axis=1, keepdims=True)
            if p < n_pages:
                vt = v_refs[s * n_pages + p][0].reshape(A_WIDTH, page).astype(BF16)
                acc = acc + _nt_dot(pr.astype(BF16), vt)
            else:
                acc = acc + jnp.dot(pr.astype(BF16), pad_rows(vn_ref[r, :]).astype(BF16),
                                    preferred_element_type=F32)
        acc = acc * (1.0 / lsum) * bd
        out = acc[0:ds]
        for h in range(1, A_HEADS):
            out = out + acc[h * ds:(h + 1) * ds]
        o_ref[r, :] = out


def _dsa_sample(p, row0, cache_kidx, cache_k, cache_v, layer, page_table, tabs, dec_batch, dec_seq):
    n_pages = page_table.shape[1]
    page = cache_k.shape[-1]
    past = n_pages * page
    topk = min(TOPK_MAX, (past + dec_seq) // 4)
    tab_last, tab_new, cfar, bd = tabs
    nb = math.gcd(dec_batch, SAMPLE_SEQS_PER_STEP)
    rows = nb * dec_seq
    rb = row0 // rows
    kern = functools.partial(_dsa_sample_kernel, n_pages=n_pages, page=page, topk=topk, nb=nb)

    def pspec(col, w):
        return pl.BlockSpec((rows, w), lambda b, pt: (rb + b, col // w))

    def page_spec(arr, s, pg):
        blk = (None, 1) + arr.shape[2:]
        zeros = (0,) * (arr.ndim - 2)
        return pl.BlockSpec(blk, lambda b, pt: (layer, pt[b * nb + s, pg]) + zeros)

    def const_spec(a):
        return pl.BlockSpec(a.shape, lambda b, pt: (0,) * a.ndim)

    in_specs = [pspec(COL_Q, A_WIDTH), pspec(COL_K, A_WIDTH), pspec(COL_V, A_WIDTH),
                pspec(COL_QI, A_WIDTH), pspec(COL_IKW, LANES)]
    for arr in (cache_kidx, cache_k, cache_v):
        in_specs += [page_spec(arr, s, g) for s in range(nb) for g in range(n_pages)]
    in_specs += [const_spec(a) for a in (tab_last, tab_new, cfar, bd)]
    grid_spec = pltpu.PrefetchScalarGridSpec(
        num_scalar_prefetch=1,
        grid=(dec_batch // nb,),
        in_specs=in_specs,
        out_specs=pl.BlockSpec((rows, A_WIDTH), lambda b, pt: (b, 0)),
    )
    npg = nb * n_pages
    args = [p] * 5 + [cache_kidx] * npg + [cache_k] * npg + [cache_v] * npg
    args += [tab_last, tab_new, cfar, bd]
    return pl.pallas_call(
        kern,
        grid_spec=grid_spec,
        out_shape=jax.ShapeDtypeStruct((dec_batch * dec_seq, A_WIDTH), F32),
        compiler_params=_cparams(("arbitrary",)),
        name="dsa_sample",
    )(page_table, *args)


def _mix_kernel(bg_ref, cg_ref, bh_ref, hf_ref, hi_ref, hq_ref, hg_ref, cs_ref, s0_ref,
                cw_ref, lb_ref, ng_ref, bdk_ref, bds_ref, cm_ref, seg_ref,
                o_ref, cso_ref, so_ref,
                uext_ref, st_ref, qi_ref, qa_ref, ka_ref, ks_ref, b_ref, oo_ref, *, chunk, nsb):
    t = pl.program_id(1)
    tt = bg_ref.shape[0] // nsb
    nblk = tt // chunk
    pad = SUBLANES
    lb = lb_ref[...]
    bdk = bdk_ref[...]
    bds = bds_ref[...]
    cmask = cm_ref[...]
    seg = seg_ref[...]
    unroll = math.gcd(nblk, HGRN_UNROLL)

    def tile_rows(x):
        return jnp.broadcast_to(x[None], (C_HEADS,) + x.shape).reshape(C_HEADS * chunk, x.shape[1])

    for s in range(nsb):
        tile = slice(s * tt, (s + 1) * tt)
        uext = uext_ref.at[s]
        st_s = st_ref.at[s]

        @pl.when(t == 0)
        def _():
            uext[pad - (CONV_K - 1):pad, :] = cs_ref[s]
            st_s[...] = jnp.zeros(st_s.shape, F32)
            for h in range(C_HEADS):
                st_s[h * C_DV:(h + 1) * C_DV, h * C_DK:(h + 1) * C_DK] = s0_ref[s, h].T

        u = cg_ref[tile, :] * bh_ref[tile, :]
        uext[pad:pad + tt, :] = u
        y = cw_ref[CONV_K - 1:CONV_K, :] * u
        for j in range(CONV_K - 1):
            y = y + cw_ref[j:j + 1, :] * uext[pad - (CONV_K - 1) + j:pad - (CONV_K - 1) + j + tt, :]
        o_ref[tile, :B_WIDTH] = bg_ref[tile, :] * y
        tail = uext[pad + tt - (CONV_K - 1):pad + tt, :]
        uext[pad - (CONV_K - 1):pad, :] = tail
        cso_ref[s] = tail

        fz = hf_ref[tile, :]
        f = lb + (1.0 - lb) * jax.nn.sigmoid(fz)
        logf = jnp.log(jnp.maximum(f, 1e-30))
        kk = (1.0 - lb) * jax.nn.sigmoid(-fz)
        qz = hq_ref[tile, :]
        qq = qz * jax.nn.sigmoid(qz)
        rowc = lax.broadcasted_iota(I32, (tt, 1), 0) % chunk
        b = logf
        d = 1
        while d < chunk:
            b = b + jnp.where(rowc >= d, pltpu.roll(b, d, 0), 0.0)
            d *= 2
        b3 = b.reshape(nblk, chunk, C_HEADS * C_DK)
        blast3 = b3[:, chunk - 1:chunk, :]
        bmid3 = b3[:, chunk // 2:chunk // 2 + 1, :]
        blast = jnp.broadcast_to(blast3, b3.shape).reshape(tt, C_HEADS * C_DK)
        bmid = jnp.broadcast_to(bmid3, b3.shape).reshape(tt, C_HEADS * C_DK)
        b_ref[tile, :] = b
        qi_ref[tile, :] = qq * jnp.exp(b)
        qa_ref[tile, :] = qq * jnp.exp(jnp.clip(b - bmid, -EXP_CLAMP, EXP_CLAMP))
        ka_ref[tile, :] = kk * jnp.exp(jnp.clip(bmid - b, -EXP_CLAMP, EXP_CLAMP))
        ks_ref[tile, :] = kk * jnp.exp(blast - b)

        def block_body(jo, st, s=s):
            for un in range(unroll):
                r0 = pl.multiple_of(s * tt + (jo * unroll + un) * chunk, chunk)
                rows = pl.ds(r0, chunk)
                v = hi_ref[rows, :]
                o_inter = _nt_dot(qi_ref[rows, :].astype(BF16), st.astype(BF16))
                kabd = (tile_rows(ka_ref[rows, :]) * bdk).astype(BF16)
                amat = _nt_dot(qa_ref[rows, :].astype(BF16), kabd) * cmask
                vbd = (tile_rows(v) * bdk).astype(BF16)
                o_intra = jnp.dot(amat.astype(BF16), vbd, preferred_element_type=F32)
                oo_ref[rows, :] = o_inter + o_intra
                dst = _tn_dot(v.astype(BF16), ks_ref[rows, :].astype(BF16))
                el = jnp.exp(b_ref[pl.ds(r0 + chunk - 1, 1), :])
                st = st * el + dst * bds
            return st

        st_s[...] = lax.fori_loop(0, nblk // unroll, block_body, st_s[...])

        o = oo_ref[tile, :]
        sq = o * o
        hi = sq.astype(BF16)
        lo = (sq - hi.astype(F32)).astype(BF16)
        ms = jnp.dot(hi, seg, preferred_element_type=F32) + jnp.dot(lo, seg, preferred_element_type=F32)
        gz = hg_ref[tile, :]
        o_ref[tile, B_WIDTH:] = o * lax.rsqrt(ms + EPS) * ng_ref[...] * (gz * jax.nn.sigmoid(gz))

        @pl.when(t == pl.num_programs(1) - 1)
        def _():
            stt = st_s[...].T
            for h in range(C_HEADS):
                so_ref[s, h] = stt[h * C_DK:(h + 1) * C_DK, h * C_DV:(h + 1) * C_DV]


def _mix(p, row0, nseq, seq, tt, chunk, nsb, conv_state, s0, cw, lb, ng, consts):
    bdk, bds, cmask, seg = consts
    nt = seq // tt
    assert nseq % nsb == 0 and (nsb == 1 or nt == 1)
    rows = nsb * tt
    rb = row0 // rows

    def pspec(col):
        return pl.BlockSpec((rows, B_WIDTH), lambda s, t: (rb + s * nt + t, col // B_WIDTH))

    def const_spec(a):
        return pl.BlockSpec(a.shape, lambda s, t: (0,) * a.ndim)

    w = C_HEADS * C_DK
    lb = lb.reshape(1, w)
    ng = ng.reshape(1, C_WIDTH)
    return pl.pallas_call(
        functools.partial(_mix_kernel, chunk=chunk, nsb=nsb),
        grid=(nseq // nsb, nt),
        in_specs=[pspec(c) for c in (COL_BG, COL_CG, COL_BH, COL_HF, COL_HI, COL_HQ, COL_HG)] + [
            pl.BlockSpec((nsb, CONV_K - 1, B_WIDTH), lambda s, t: (s, 0, 0)),
            pl.BlockSpec((nsb, C_HEADS, C_DK, C_DV), lambda s, t: (s, 0, 0, 0)),
            const_spec(cw), const_spec(lb), const_spec(ng),
            const_spec(bdk), const_spec(bds), const_spec(cmask), const_spec(seg),
        ],
        out_specs=[
            pl.BlockSpec((rows, B_WIDTH + C_WIDTH), lambda s, t: (s * nt + t, 0)),
            pl.BlockSpec((nsb, CONV_K - 1, B_WIDTH), lambda s, t: (s, 0, 0)),
            pl.BlockSpec((nsb, C_HEADS, C_DK, C_DV), lambda s, t: (s, 0, 0, 0)),
        ],
        out_shape=[
            jax.ShapeDtypeStruct((nseq * seq, B_WIDTH + C_WIDTH), F32),
            jax.ShapeDtypeStruct((nseq, CONV_K - 1, B_WIDTH), F32),
            jax.ShapeDtypeStruct((nseq, C_HEADS, C_DK, C_DV), F32),
        ],
        scratch_shapes=[
            pltpu.VMEM((nsb, tt + SUBLANES, B_WIDTH), F32),
            pltpu.VMEM((nsb, C_WIDTH, w), F32),
            pltpu.VMEM((rows, w), F32),
            pltpu.VMEM((rows, w), F32),
            pltpu.VMEM((rows, w), F32),
            pltpu.VMEM((rows, w), F32),
            pltpu.VMEM((rows, w), F32),
            pltpu.VMEM((rows, C_WIDTH), F32),
        ],
        compiler_params=_cparams(("arbitrary", "arbitrary")),
        name="mix_c%d" % chunk,
    )(p, p, p, p, p, p, p, conv_state, s0, cw, lb, ng, bdk, bds, cmask, seg)


def _mix_consts(chunk):
    w = C_HEADS * C_DK
    rh = jnp.arange(C_HEADS * chunk)[:, None] // chunk
    ch = jnp.arange(w)[None, :] // C_DK
    bdk = (rh == ch).astype(F32)
    hv = jnp.arange(C_WIDTH)[:, None] // C_DV
    bds = (hv == ch).astype(F32)
    tq = jnp.arange(chunk)[:, None]
    sk = jnp.arange(C_HEADS * chunk)[None, :] % chunk
    cmask = (sk <= tq).astype(F32)
    seg = (bds / C_DV).astype(BF16)
    return bdk, bds, cmask, seg


def _t5_bucket(dist):
    dist = jnp.maximum(dist, 0)
    max_exact = N_BUCKETS // 2
    d = jnp.maximum(dist, 1).astype(F32)
    large = max_exact + (jnp.log(d / max_exact) / math.log(MAX_DISTANCE / max_exact)
                         * (N_BUCKETS - max_exact)).astype(I32)
    large = jnp.clip(large, 0, N_BUCKETS - 1)
    return jnp.where(dist < max_exact, dist, large)


def _bias_of_dist(rel_bias, dist):
    bucket = _t5_bucket(dist)
    table = rel_bias.astype(F32) * LOG2E
    expand = (1,) * dist.ndim
    bias = jnp.zeros((A_HEADS,) + dist.shape, F32)
    for b in range(N_BUCKETS):
        bias = jnp.where((bucket == b)[None], table[b].reshape((A_HEADS,) + expand), bias)
    return jnp.where((dist >= 0)[None], bias, NEG_BIG)


def _far_bias(rel_bias, min_dist):
    assert int(16 + math.log(min_dist / 16) / math.log(MAX_DISTANCE / 16) * 16) >= N_BUCKETS - 1
    return rel_bias.astype(F32)[N_BUCKETS - 1] * LOG2E


def _permute_w_in(w_in):
    sizes = (A_WIDTH, A_WIDTH, A_WIDTH, IDX_HEADS * IDX_DIM, IDX_DIM, IDX_HEADS,
             B_WIDTH, B_WIDTH, B_WIDTH, C_HEADS * C_DK, C_WIDTH, C_HEADS * C_DK, C_WIDTH)
    offs = [0]
    for s in sizes:
        offs.append(offs[-1] + s)
    cols = lambda i: w_in[..., offs[i]:offs[i + 1]]
    padw = LANES - IDX_DIM - IDX_HEADS
    pad = jnp.zeros(w_in.shape[:-1] + (padw,), w_in.dtype)
    parts = [cols(i) for i in (0, 1, 2, 3, 6, 7, 8, 9, 10, 11, 12)] + [cols(4), cols(5), pad]
    return jnp.concatenate(parts, axis=-1).astype(BF16)


def kernel(x_prompt, x_sample, cache_k, cache_v, cache_kidx, state_conv, state_hgrn, page_table,
           w_in, w_out, conv_w, hgrn_lb_logits, hgrn_norm_g, rel_bias, norm1_g, norm2_g,
           w_ff1, w_ff2, final_g):
    batch, seq, d_model = x_prompt.shape
    dec_batch, dec_seq, _ = x_sample.shape
    depth = w_in.shape[0]
    n_pool, page = cache_k.shape[1], cache_k.shape[2]
    n_pages = page_table.shape[1]
    past = n_pages * page
    n_p, n_s = batch * seq, dec_batch * dec_seq
    n = n_p + n_s
    assert dec_seq == SUBLANES and seq % (2 * TQ) == 0 and seq % SCORE_CHUNK == 0
    tm = 512 if n % 512 == 0 else 256
    assert n % tm == 0 and n_p % tm == 0

    x = jnp.concatenate([x_prompt.reshape(n_p, d_model), x_sample.reshape(n_s, d_model)], axis=0)
    w_in_p = _permute_w_in(w_in)
    w_out_b = w_out.astype(BF16)
    w_ff1_b = w_ff1.astype(BF16)
    w_ff2_b = w_ff2.astype(BF16)
    sm = jax.nn.softmax(hgrn_lb_logits.astype(F32), axis=0)
    lb_all = jnp.cumsum(sm, axis=0) - sm[0]

    ck = jnp.transpose(cache_k, (0, 1, 3, 4, 2))
    cv = jnp.transpose(cache_v, (0, 1, 3, 4, 2))
    cki = jnp.transpose(cache_kidx, (0, 1, 3, 2))

    qi_ = jnp.arange(TQ)[None, :]
    kj_ = jnp.arange(2 * TQ)[:, None]
    tab_p = jnp.stack([_bias_of_dist(rel_bias, qi_ - kj_), _bias_of_dist(rel_bias, TQ + qi_ - kj_)])
    cfar_p = _far_bias(rel_bias, TQ + 1)
    rows = A_HEADS * dec_seq
    si = jnp.arange(dec_seq)[:, None]
    sj = jnp.arange(LANES)[None, :]
    tab_last = _bias_of_dist(rel_bias, page + si - sj).reshape(rows, LANES)
    d_new = jnp.where(sj < dec_seq, si - sj, -1)
    tab_new = _bias_of_dist(rel_bias, d_new).reshape(rows, LANES)
    cfar_s = jnp.broadcast_to(jnp.repeat(_far_bias(rel_bias, page + 1), dec_seq)[:, None], (rows, LANES))
    bd_s = (jnp.arange(rows)[:, None] // dec_seq == jnp.arange(A_WIDTH)[None, :] // HEAD_DIM).astype(F32)
    tabs_s = (tab_last, tab_new, cfar_s, bd_s)

    chunk_p = math.gcd(seq, HGRN_CHUNK)
    chunk_s = math.gcd(dec_seq, HGRN_CHUNK)
    tt_p = 512 if seq % 512 == 0 else seq
    nsb_s = math.gcd(dec_batch, MIX_SEQS_PER_STEP)
    consts_p = _mix_consts(chunk_p)
    consts_s = _mix_consts(chunk_s)
    conv0 = jnp.zeros((batch, CONV_K - 1, B_WIDTH), F32)
    hgrn0 = jnp.zeros((batch, C_HEADS, C_DK, C_DV), F32)

    kbuf = jnp.zeros((depth, batch, A_WIDTH, seq), F32)
    vbuf = jnp.zeros((depth, batch, A_WIDTH, seq), F32)
    ibuf = jnp.zeros((depth, batch, IDX_DIM, seq), F32)

    ks, vs, kis, cps, sps, css, sss = [], [], [], [], [], [], []
    for l in range(depth):
        p, qt, qit, kb, vt, ikwb, iwt, kbuf, vbuf, ibuf = _inproj(
            x, norm1_g[l], w_in_p[l], tm, l, kbuf, vbuf, ibuf)
        a_p = _dsa_prompt(iwt, qt, qit, kb, ikwb, vt, tab_p, cfar_p, batch, seq)
        a_s = _dsa_sample(p, n_p, cki, ck, cv, l, page_table, tabs_s, dec_batch, dec_seq)
        bc_p, conv_p, s_p = _mix(p, 0, batch, seq, tt_p, chunk_p, 1, conv0, hgrn0,
                                 conv_w[l], lb_all[l], hgrn_norm_g[l], consts_p)
        bc_s, conv_s, s_s = _mix(p, n_p, dec_batch, dec_seq, dec_seq, chunk_s, nsb_s, state_conv[l],
                                 state_hgrn[l], conv_w[l], lb_all[l], hgrn_norm_g[l], consts_s)
        x = _outffn(x, a_p, bc_p, a_s, bc_s, w_out_b[l], norm2_g[l], w_ff1_b[l], w_ff2_b[l], final_g,
                    tm, 1024, l == depth - 1)
        ks.append(p[n_p:, COL_K:COL_K + A_WIDTH])
        vs.append(p[n_p:, COL_V:COL_V + A_WIDTH])
        kis.append(p[n_p:, COL_IKW:COL_IKW + IDX_DIM])
        cps.append(conv_p); sps.append(s_p); css.append(conv_s); sss.append(s_s)

    heads = lambda buf: jnp.transpose(buf.reshape(depth, batch, A_HEADS, HEAD_DIM, seq), (0, 1, 4, 2, 3))
    return (
        x[:n_p].reshape(batch, seq, d_model),
        x[n_p:].reshape(dec_batch, dec_seq, d_model),
        heads(kbuf), heads(vbuf), jnp.transpose(ibuf, (0, 1, 3, 2)),
        jnp.stack(cps), jnp.stack(sps),
        jnp.stack(ks).reshape(depth, dec_batch, dec_seq, A_HEADS, HEAD_DIM),
        jnp.stack(vs).reshape(depth, dec_batch, dec_seq, A_HEADS, HEAD_DIM),
        jnp.stack(kis).reshape(depth, dec_batch, dec_seq, IDX_DIM),
        jnp.stack(css), jnp.stack(sss),
    )
```

```python
import functools
import math

import jax
import jax.numpy as jnp
from jax import lax
from jax.experimental import pallas as pl
from jax.experimental.pallas import tpu as pltpu

F32 = jnp.float32
BF16 = jnp.bfloat16
I32 = jnp.int32

HEAD_DIM = 64
A_HEADS = 8
A_WIDTH = A_HEADS * HEAD_DIM
IDX_HEADS = 8
IDX_DIM = 64
B_WIDTH = 256
C_HEADS = 4
C_DK = 64
C_DV = 64
C_WIDTH = C_HEADS * C_DV
TOPK_MAX = 256
CONV_K = 3
N_BUCKETS = 32
MAX_DISTANCE = 128
HGRN_CHUNK = 16
HGRN_UNROLL = 4
BISECT_GROUP = 4
MIX_SEQS_PER_STEP = 8
SAMPLE_SEQS_PER_STEP = 2
EPS = 1e-6
NEG_BIG = -1e30
LOG2E = 1.4426950408889634
INT_MIN = -(2 ** 31)
INT_MAX = 2 ** 31 - 1
EXP_CLAMP = 80.0

LANES = 128
SUBLANES = 8
VMEM_LIMIT = 56 * 1024 * 1024

COL_Q, COL_K, COL_V, COL_QI = 0, 512, 1024, 1536
COL_BG, COL_CG, COL_BH, COL_HF, COL_HI, COL_HQ, COL_HG = (2048 + 256 * i for i in range(7))
COL_IKW = 3840
P_WIDTH = 3968

TQ = 128
SCORE_CHUNK = 512
ATT_CHUNK = 512


def _cparams(sem):
    return pltpu.CompilerParams(dimension_semantics=sem, vmem_limit_bytes=VMEM_LIMIT)


def _nt_dot(a, b):
    return lax.dot_general(a, b, (((1,), (1,)), ((), ())), preferred_element_type=F32)


def _tn_dot(a, b):
    return lax.dot_general(a, b, (((0,), (0,)), ((), ())), preferred_element_type=F32)


def _float_key(x):
    bits = pltpu.bitcast(x, I32)
    return bits ^ ((bits >> 31) & INT_MAX)


def _inproj_kernel(x_ref, g_ref, w_ref, ko_in, vo_in, io_in,
                   p_ref, qt_ref, qit_ref, kb_ref, vt_ref, ikw_ref, iwt_ref, ko_ref, vo_ref, io_ref,
                   *, n_prompt_blocks):
    del ko_in, vo_in, io_in
    x = x_ref[...]
    ms = jnp.mean(x * x, axis=-1, keepdims=True)
    xn = (x * lax.rsqrt(ms + EPS) * g_ref[...]).astype(BF16)
    res = jnp.dot(xn, w_ref[...], preferred_element_type=F32)
    p_ref[...] = res
    q_scale = (HEAD_DIM ** -0.5) * LOG2E
    w_scale = (IDX_HEADS ** -0.5) * (IDX_DIM ** -0.5)
    qt_ref[...] = (res[:, COL_Q:COL_Q + A_WIDTH] * q_scale).T.astype(BF16)
    qit_ref[...] = res[:, COL_QI:COL_QI + A_WIDTH].T.astype(BF16)
    k = res[:, COL_K:COL_K + A_WIDTH]
    kb_ref[...] = k.astype(BF16)
    vt = res[:, COL_V:COL_V + A_WIDTH].T
    vt_ref[...] = vt.astype(BF16)
    ikw = res[:, COL_IKW:COL_IKW + LANES]
    ikw_ref[...] = ikw.astype(BF16)
    ikwt = ikw.T
    iwt_ref[...] = ikwt[IDX_DIM:IDX_DIM + IDX_HEADS] * w_scale

    @pl.when(pl.program_id(0) < n_prompt_blocks)
    def _():
        ko_ref[...] = k.T
        vo_ref[...] = vt
        io_ref[...] = ikwt[:IDX_DIM]


def _inproj(x, g, w, tm, layer, kbuf, vbuf, ibuf):
    n, d = x.shape
    batch, seq = kbuf.shape[1], kbuf.shape[3]
    tpb = seq // tm
    npb = batch * tpb

    def slab_spec(width):
        def index(i):
            ii = jnp.minimum(i, npb - 1)
            return (layer, ii // tpb, 0, ii % tpb)
        return pl.BlockSpec((None, None, width, tm), index)

    any_spec = pl.BlockSpec(memory_space=pl.ANY)
    return pl.pallas_call(
        functools.partial(_inproj_kernel, n_prompt_blocks=npb),
        grid=(n // tm,),
        in_specs=[
            pl.BlockSpec((tm, d), lambda i: (i, 0)),
            pl.BlockSpec((1, d), lambda i: (0, 0)),
            pl.BlockSpec((d, P_WIDTH), lambda i: (0, 0)),
            any_spec, any_spec, any_spec,
        ],
        input_output_aliases={3: 7, 4: 8, 5: 9},
        out_specs=[
            pl.BlockSpec((tm, P_WIDTH), lambda i: (i, 0)),
            pl.BlockSpec((A_WIDTH, tm), lambda i: (0, i)),
            pl.BlockSpec((A_WIDTH, tm), lambda i: (0, i)),
            pl.BlockSpec((tm, A_WIDTH), lambda i: (i, 0)),
            pl.BlockSpec((A_WIDTH, tm), lambda i: (0, i)),
            pl.BlockSpec((tm, LANES), lambda i: (i, 0)),
            pl.BlockSpec((IDX_HEADS, tm), lambda i: (0, i)),
            slab_spec(A_WIDTH), slab_spec(A_WIDTH), slab_spec(IDX_DIM),
        ],
        out_shape=[
            jax.ShapeDtypeStruct((n, P_WIDTH), F32),
            jax.ShapeDtypeStruct((A_WIDTH, n), BF16),
            jax.ShapeDtypeStruct((A_WIDTH, n), BF16),
            jax.ShapeDtypeStruct((n, A_WIDTH), BF16),
            jax.ShapeDtypeStruct((A_WIDTH, n), BF16),
            jax.ShapeDtypeStruct((n, LANES), BF16),
            jax.ShapeDtypeStruct((IDX_HEADS, n), F32),
            jax.ShapeDtypeStruct(kbuf.shape, F32),
            jax.ShapeDtypeStruct(vbuf.shape, F32),
            jax.ShapeDtypeStruct(ibuf.shape, F32),
        ],
        compiler_params=_cparams(("arbitrary",)),
        name="inproj",
    )(x, g.reshape(1, d), w, kbuf, vbuf, ibuf)


def _outffn_kernel(x_ref, ap_ref, bcp_ref, as_ref, bcs_ref, wo_ref, g2_ref, w1_ref, w2_ref, gf_ref,
                   o_ref, acc_ref, hn_ref, *, final, n_prompt_blocks):
    i = pl.program_id(0)
    j = pl.program_id(1)

    def mix_in(a_ref, bc_ref):
        x1 = x_ref[...]
        x1 = x1 + jnp.dot(a_ref[...].astype(BF16), wo_ref[:A_WIDTH, :], preferred_element_type=F32)
        x1 = x1 + jnp.dot(bc_ref[...].astype(BF16), wo_ref[A_WIDTH:, :], preferred_element_type=F32)
        acc_ref[...] = x1
        ms = jnp.mean(x1 * x1, axis=-1, keepdims=True)
        hn_ref[...] = (x1 * lax.rsqrt(ms + EPS) * g2_ref[...]).astype(BF16)

    @pl.when(jnp.logical_and(j == 0, i < n_prompt_blocks))
    def _():
        mix_in(ap_ref, bcp_ref)

    @pl.when(jnp.logical_and(j == 0, i >= n_prompt_blocks))
    def _():
        mix_in(as_ref, bcs_ref)

    h = jnp.dot(hn_ref[...], w1_ref[...], preferred_element_type=F32)
    h = jnp.maximum(h, 0.0)
    h = (h * h).astype(BF16)
    acc_ref[...] += jnp.dot(h, w2_ref[...], preferred_element_type=F32)

    @pl.when(j == pl.num_programs(1) - 1)
    def _():
        y = acc_ref[...]
        if final:
            ms = jnp.mean(y * y, axis=-1, keepdims=True)
            y = y * lax.rsqrt(ms + EPS) * gf_ref[...]
        o_ref[...] = y


def _outffn(x, a_p, bc_p, a_s, bc_s, wo, g2, w1, w2, gf, tm, tf, final):
    n, d = x.shape
    dff = w1.shape[1]
    npb = a_p.shape[0] // tm
    prompt_rows = lambda i, j: (jnp.minimum(i, npb - 1), 0)
    sample_rows = lambda i, j: (jnp.maximum(i - npb, 0), 0)
    return pl.pallas_call(
        functools.partial(_outffn_kernel, final=final, n_prompt_blocks=npb),
        grid=(n // tm, dff // tf),
        in_specs=[
            pl.BlockSpec((tm, d), lambda i, j: (i, 0)),
            pl.BlockSpec((tm, A_WIDTH), prompt_rows),
            pl.BlockSpec((tm, B_WIDTH + C_WIDTH), prompt_rows),
            pl.BlockSpec((tm, A_WIDTH), sample_rows),
            pl.BlockSpec((tm, B_WIDTH + C_WIDTH), sample_rows),
            pl.BlockSpec(wo.shape, lambda i, j: (0, 0)),
            pl.BlockSpec((1, d), lambda i, j: (0, 0)),
            pl.BlockSpec((d, tf), lambda i, j: (0, j)),
            pl.BlockSpec((tf, d), lambda i, j: (j, 0)),
            pl.BlockSpec((1, d), lambda i, j: (0, 0)),
        ],
        out_specs=pl.BlockSpec((tm, d), lambda i, j: (i, 0)),
        out_shape=jax.ShapeDtypeStruct((n, d), F32),
        scratch_shapes=[pltpu.VMEM((tm, d), F32), pltpu.VMEM((tm, d), BF16)],
        compiler_params=_cparams(("arbitrary", "arbitrary")),
        name="outffn_final" if final else "outffn",
    )(x, a_p, bc_p, a_s, bc_s, wo, g2.reshape(1, d), w1, w2, gf.reshape(1, d))


def _topk_threshold(count_ge, count_eq_before, like, topk, n_pos):
    kf = float(topk)
    ans = _bisect_keys(count_ge, like, topk)
    c_ans = count_ge(ans)
    pos_bits = max(1, (n_pos - 1).bit_length())

    def tie_cut(_):
        need = kf - count_ge(ans + 1)

        def jbody(i, xcut):
            cand = xcut + (jnp.int32(1) << (pos_bits - 1 - i))
            return jnp.where(count_eq_before(ans, cand) < need, cand, xcut)

        cut = lax.fori_loop(0, pos_bits, jbody, jnp.zeros_like(like))
        return jnp.where(c_ans > kf, cut, n_pos)

    jcut = lax.cond(jnp.max(c_ans) > kf, tie_cut, lambda _: jnp.full_like(like, n_pos), 0)
    return ans, jcut


def _bisect_keys(count_ge, like, topk):
    kf = float(topk)
    c0 = count_ge(jnp.zeros_like(like))
    ans0 = jnp.where(c0 >= kf, 0, INT_MIN).astype(I32)
    open0 = jnp.where(c0 == kf, 0.0, 1.0)

    def step(i, ans, opn):
        cand = ans + (jnp.int32(1) << (30 - i))
        c = count_ge(cand)
        take = jnp.logical_and(c >= kf, opn > 0.0)
        return jnp.where(take, cand, ans), jnp.where(c == kf, 0.0, opn)

    def group(st):
        g, ans, opn = st
        for u in range(BISECT_GROUP):
            ans, opn = step(g * BISECT_GROUP + u, ans, opn)
        return g + 1, ans, opn

    def pending(st):
        g, _, opn = st
        return jnp.logical_and(g < 31 // BISECT_GROUP, jnp.max(opn) > 0.0)

    g, ans, opn = lax.while_loop(pending, group, (jnp.int32(0), ans0, open0))

    def tail(_):
        return lax.fori_loop(g * BISECT_GROUP, 31, lambda i, st: step(i, *st), (ans, opn))[0]

    return lax.cond(jnp.max(opn) > 0.0, tail, lambda _: ans, 0)


def _dsa_prompt_kernel(cfar_ref, iwt_ref, qt_ref, qit_ref, kb_ref, ikw_ref, vt_ref, tab_ref, o_ref,
                       sc_ref, qbd_ref, qibd_ref, acc0_ref, acc1_ref, acc2_ref, acc3_ref,
                       ans_ref, jcut_ref, *, seq, topk):
    qb = pl.program_id(1)
    tq = TQ
    qs = qb * tq
    npair = A_HEADS // 2

    zq = jnp.zeros((HEAD_DIM, tq), BF16)
    for j in range(npair):
        top = qt_ref[(2 * j) * HEAD_DIM:(2 * j + 1) * HEAD_DIM, :]
        bot = qt_ref[(2 * j + 1) * HEAD_DIM:(2 * j + 2) * HEAD_DIM, :]
        qbd_ref[j] = jnp.concatenate([jnp.concatenate([top, zq], axis=1),
                                      jnp.concatenate([zq, bot], axis=1)], axis=0)
        itop = qit_ref[(2 * j) * IDX_DIM:(2 * j + 1) * IDX_DIM, :]
        ibot = qit_ref[(2 * j + 1) * IDX_DIM:(2 * j + 2) * IDX_DIM, :]
        qibd_ref[j] = jnp.concatenate([jnp.concatenate([itop, ibot], axis=1),
                                       jnp.zeros((LANES - IDX_DIM, 2 * tq), BF16)], axis=0)

    wrow = iwt_ref[...]
    tpos = qs + lax.broadcasted_iota(I32, (1, tq), 1)

    def fold(x, op):
        while x.shape[0] > SUBLANES:
            half = x.shape[0] // 2
            x = op(x[:half], x[half:])
        return x

    n_sc = (qs + tq + SCORE_CHUNK - 1) // SCORE_CHUNK

    def score_body(c, carry):
        c0 = pl.multiple_of(c * SCORE_CHUNK, SCORE_CHUNK)
        ikc = ikw_ref[pl.ds(c0, SCORE_CHUNK), :]
        score = jnp.zeros((SCORE_CHUNK, tq), F32)
        for j in range(npair):
            s = jnp.dot(ikc, qibd_ref[j], preferred_element_type=F32)
            score = score + jnp.maximum(s[:, :tq], 0.0) * wrow[2 * j:2 * j + 1, :]
            score = score + jnp.maximum(s[:, tq:], 0.0) * wrow[2 * j + 1:2 * j + 2, :]
        pos = c0 + lax.broadcasted_iota(I32, (SCORE_CHUNK, tq), 0)
        sc_ref[pl.ds(c0, SCORE_CHUNK), :] = _float_key(jnp.where(pos <= tpos, score, NEG_BIG))
        return carry

    lax.fori_loop(0, n_sc, score_body, 0)

    def count_ge(cand):
        def body(c, acc):
            c0 = pl.multiple_of(c * SCORE_CHUNK, SCORE_CHUNK)
            keys = sc_ref[pl.ds(c0, SCORE_CHUNK), :]
            return acc + fold(jnp.where(keys >= cand, 1.0, 0.0), jnp.add)
        acc = lax.fori_loop(0, n_sc, body, jnp.zeros((SUBLANES, tq), F32))
        return jnp.sum(acc, axis=0, keepdims=True)

    def count_eq_before(a, xcut):
        def body(c, acc):
            c0 = pl.multiple_of(c * SCORE_CHUNK, SCORE_CHUNK)
            keys = sc_ref[pl.ds(c0, SCORE_CHUNK), :]
            pos = c0 + lax.broadcasted_iota(I32, (SCORE_CHUNK, tq), 0)
            hit = jnp.where(keys == a, jnp.where(pos < xcut, 1.0, 0.0), 0.0)
            return acc + fold(hit, jnp.add)
        acc = lax.fori_loop(0, n_sc, body, jnp.zeros((SUBLANES, tq), F32))
        return jnp.sum(acc, axis=0, keepdims=True)

    ans_ref[...] = jnp.full((1, tq), INT_MIN + 1, I32)
    jcut_ref[...] = jnp.full((1, tq), seq, I32)

    @pl.when(qs + tq > topk)
    def _():
        ans, jcut = _topk_threshold(count_ge, count_eq_before, jnp.zeros((1, tq), I32), topk, seq)
        ans_ref[...] = ans
        jcut_ref[...] = jcut

    acc_refs = (acc0_ref, acc1_ref, acc2_ref, acc3_ref)
    for r in acc_refs:
        r[...] = jnp.zeros(r.shape, F32)
    ans = ans_ref[...]
    jcut = jcut_ref[...]

    def attend_chunk(c0, width, limit, tsel, ms, lss):
        keys = sc_ref[pl.ds(c0, width), :]
        pos = c0 + lax.broadcasted_iota(I32, (width, tq), 0)
        thr = jnp.where(pos <= jcut, ans - 1, ans)
        thr = jnp.where(pos < limit, thr, INT_MAX)
        sel = keys > thr
        lgs = [jnp.dot(kb_ref[pl.ds(c0, width), j * LANES:(j + 1) * LANES], qbd_ref[j],
                       preferred_element_type=F32) for j in range(npair)]
        ms_new, lss_new = [], []
        for j in range(npair):
            ps, alphas = [], []
            for half in range(2):
                h = 2 * j + half
                lh = lgs[j][:, half * tq:(half + 1) * tq]
                if tsel is None:
                    cb = cfar_ref[h]
                else:
                    lh = lh + tab_ref[tsel, h]
                    cb = 0.0
                lh = jnp.where(sel, lh, NEG_BIG)
                cmax = jnp.max(fold(lh, jnp.maximum), axis=0, keepdims=True)
                m_new = jnp.maximum(ms[h], cmax + cb)
                alpha = jnp.exp2(ms[h] - m_new)
                p = jnp.exp2(lh - (m_new - cb))
                ms_new.append(m_new)
                lss_new.append(alpha * lss[h] + fold(p, jnp.add))
                ps.append(p.astype(BF16))
                alphas.append(alpha)
            vt = vt_ref[j * LANES:(j + 1) * LANES, pl.ds(c0, width)]
            pv = jnp.dot(vt, jnp.concatenate(ps, axis=1), preferred_element_type=F32)
            acc_refs[j][...] = acc_refs[j][...] * jnp.concatenate(alphas, axis=1) + pv
        return tuple(ms_new), tuple(lss_new)

    near0 = jnp.maximum(qb - 1, 0) * tq

    def far_body(c, carry):
        return attend_chunk(pl.multiple_of(c * ATT_CHUNK, ATT_CHUNK), ATT_CHUNK, near0, None, *carry)

    init = (tuple(jnp.full((1, tq), NEG_BIG, F32) for _ in range(A_HEADS)),
            tuple(jnp.zeros((SUBLANES, tq), F32) for _ in range(A_HEADS)))
    carry = lax.fori_loop(0, (near0 + ATT_CHUNK - 1) // ATT_CHUNK, far_body, init)
    _, lss = attend_chunk(pl.multiple_of(near0, tq), 2 * tq, seq, jnp.minimum(qb, 1), *carry)

    for j in range(npair):
        outs = []
        for half in range(2):
            inv = 1.0 / jnp.sum(lss[2 * j + half], axis=0, keepdims=True)
            blk = acc_refs[j][half * HEAD_DIM:(half + 1) * HEAD_DIM, half * tq:(half + 1) * tq]
            outs.append((blk * inv).T)
        o_ref[:, j * LANES:(j + 1) * LANES] = jnp.concatenate(outs, axis=1)


def _dsa_prompt(iwt, qt, qit, kb, ikwb, vt, tab, cfar, batch, seq):
    tq = TQ
    nq = seq // tq
    topk = min(TOPK_MAX, seq // 4)
    kern = functools.partial(_dsa_prompt_kernel, seq=seq, topk=topk)
    return pl.pallas_call(
        kern,
        grid=(batch, nq),
        in_specs=[
            pl.BlockSpec(memory_space=pltpu.SMEM),
            pl.BlockSpec((IDX_HEADS, tq), lambda b, i: (0, b * nq + i)),
            pl.BlockSpec((A_WIDTH, tq), lambda b, i: (0, b * nq + i)),
            pl.BlockSpec((A_WIDTH, tq), lambda b, i: (0, b * nq + i)),
            pl.BlockSpec((seq, A_WIDTH), lambda b, i: (b, 0)),
            pl.BlockSpec((seq, LANES), lambda b, i: (b, 0)),
            pl.BlockSpec((A_WIDTH, seq), lambda b, i: (0, b)),
            pl.BlockSpec(tab.shape, lambda b, i: (0, 0, 0, 0)),
        ],
        out_specs=pl.BlockSpec((tq, A_WIDTH), lambda b, i: (b * nq + i, 0)),
        out_shape=jax.ShapeDtypeStruct((batch * seq, A_WIDTH), F32),
        scratch_shapes=[
            pltpu.VMEM((seq, tq), I32),
            pltpu.VMEM((A_HEADS // 2, 2 * HEAD_DIM, 2 * tq), BF16),
            pltpu.VMEM((IDX_HEADS // 2, LANES, 2 * tq), BF16),
            pltpu.VMEM((2 * HEAD_DIM, 2 * tq), F32),
            pltpu.VMEM((2 * HEAD_DIM, 2 * tq), F32),
            pltpu.VMEM((2 * HEAD_DIM, 2 * tq), F32),
            pltpu.VMEM((2 * HEAD_DIM, 2 * tq), F32),
            pltpu.VMEM((1, tq), I32),
            pltpu.VMEM((1, tq), I32),
        ],
        compiler_params=_cparams(("arbitrary", "arbitrary")),
        name="dsa_prompt",
    )(cfar, iwt, qt, qit, kb, ikwb, vt, tab)


def _dsa_sample_kernel(pt_ref, q_ref, kn_ref, vn_ref, qi_ref, ikw_ref, *rest, n_pages, page, topk, nb):
    npg = nb * n_pages
    kidx_refs, k_refs, v_refs = rest[:npg], rest[npg:2 * npg], rest[2 * npg:3 * npg]
    tab_last_ref, tab_new_ref, cfar_ref, bd_ref, o_ref = rest[3 * npg:]
    ds = q_ref.shape[0] // nb
    rows = A_HEADS * ds
    past = n_pages * page
    width = past + LANES
    w_scale = (IDX_HEADS ** -0.5) * (IDX_DIM ** -0.5)
    q_scale = (HEAD_DIM ** -0.5) * LOG2E

    def tile_heads(x):
        return jnp.broadcast_to(x[None], (A_HEADS,) + x.shape).reshape(rows, x.shape[1])

    def pad_rows(x):
        return jnp.concatenate([x, jnp.zeros((LANES - ds, x.shape[1]), x.dtype)], axis=0)

    qrow = lax.broadcasted_iota(I32, (ds, LANES), 0)
    kcol = lax.broadcasted_iota(I32, (ds, LANES), 1)

    def seq_scores(s):
        r = slice(s * ds, (s + 1) * ds)
        qi = qi_ref[r, :]
        q2 = jnp.concatenate([qi[:, h * IDX_DIM:(h + 1) * IDX_DIM] for h in range(IDX_HEADS)],
                             axis=0).astype(BF16)
        wi = ikw_ref[r, IDX_DIM:IDX_DIM + IDX_HEADS] * w_scale
        wcols = [jnp.broadcast_to(wi[:, h:h + 1], (ds, LANES)) for h in range(IDX_HEADS)]

        def head_sum(sc):
            out = jnp.zeros((ds, LANES), F32)
            for h in range(IDX_HEADS):
                out = out + jnp.maximum(sc[h * ds:(h + 1) * ds], 0.0) * wcols[h]
            return out

        scores = [head_sum(jnp.dot(q2, kidx_refs[s * n_pages + p][0].astype(BF16),
                                   preferred_element_type=F32)) for p in range(n_pages)]
        ik_new = pad_rows(ikw_ref[r, :IDX_DIM]).astype(BF16)
        s_new = jnp.where(kcol <= qrow, head_sum(_nt_dot(q2, ik_new)), NEG_BIG)
        return jnp.concatenate(scores + [s_new], axis=1)

    keys = _float_key(jnp.concatenate([seq_scores(s) for s in range(nb)], axis=0))
    pos = lax.broadcasted_iota(I32, (1, width), 1)

    def count_ge(cand):
        return jnp.sum(jnp.where(keys >= cand, 1.0, 0.0), axis=1, keepdims=True)

    def count_eq_before(a, xcut):
        hit = jnp.where(keys == a, jnp.where(pos < xcut, 1.0, 0.0), 0.0)
        return jnp.sum(hit, axis=1, keepdims=True)

    ans, jcut = _topk_threshold(count_ge, count_eq_before, jnp.zeros((nb * ds, 1), I32), topk, width)
    thr = jnp.where(pos <= jcut, ans - 1, ans)
    addmask_all = jnp.where(keys > thr, 0.0, NEG_BIG)

    bd = bd_ref[...]
    for s in range(nb):
        r = slice(s * ds, (s + 1) * ds)
        addmask = addmask_all[r]
        qbd = (tile_heads(q_ref[r, :] * q_scale) * bd).astype(BF16)
        logits = []
        for p in range(n_pages):
            kt = k_refs[s * n_pages + p][0].reshape(A_WIDTH, page).astype(BF16)
            lg = jnp.dot(qbd, kt, preferred_element_type=F32)
            lg = lg + (tab_last_ref[...] if p == n_pages - 1 else cfar_ref[...])
            logits.append(lg + tile_heads(addmask[:, p * page:(p + 1) * page]))
        lg = _nt_dot(qbd, pad_rows(kn_ref[r, :]).astype(BF16)) + tab_new_ref[...]
        logits.append(lg + tile_heads(addmask[:, past:]))
        m = functools.reduce(jnp.maximum, [jnp.max(l, axis=1, keepdims=True) for l in logits])
        acc = jnp.zeros((rows, A_WIDTH), F32)
        lsum = jnp.zeros((rows, 1), F32)
        for p in range(n_pages + 1):
            pr = jnp.exp2(logits[p] - m)
            lsum = lsum + jnp.sum(pr, axis=1, keepdims=True)
            if p < n_pages:
                vt = v_refs[s * n_pages + p][0].reshape(A_WIDTH, page).astype(BF16)
                acc = acc + _nt_dot(pr.astype(BF16), vt)
            else:
                acc = acc + jnp.dot(pr.astype(BF16), pad_rows(vn_ref[r, :]).astype(BF16),
                                    preferred_element_type=F32)
        acc = acc * (1.0 / lsum) * bd
        out = acc[0:ds]
        for h in range(1, A_HEADS):
            out = out + acc[h * ds:(h + 1) * ds]
        o_ref[r, :] = out


def _dsa_sample(p, row0, cache_kidx, cache_k, cache_v, layer, page_table, tabs, dec_batch, dec_seq):
    n_pages = page_table.shape[1]
    page = cache_k.shape[-1]
    past = n_pages * page
    topk = min(TOPK_MAX, (past + dec_seq) // 4)
    tab_last, tab_new, cfar, bd = tabs
    nb = math.gcd(dec_batch, SAMPLE_SEQS_PER_STEP)
    rows = nb * dec_seq
    rb = row0 // rows
    kern = functools.partial(_dsa_sample_kernel, n_pages=n_pages, page=page, topk=topk, nb=nb)

    def pspec(col, w):
        return pl.BlockSpec((rows, w), lambda b, pt: (rb + b, col // w))

    def page_spec(arr, s, pg):
        blk = (None, 1) + arr.shape[2:]
        zeros = (0,) * (arr.ndim - 2)
        return pl.BlockSpec(blk, lambda b, pt: (layer, pt[b * nb + s, pg]) + zeros)

    def const_spec(a):
        return pl.BlockSpec(a.shape, lambda b, pt: (0,) * a.ndim)

    in_specs = [pspec(COL_Q, A_WIDTH), pspec(COL_K, A_WIDTH), pspec(COL_V, A_WIDTH),
                pspec(COL_QI, A_WIDTH), pspec(COL_IKW, LANES)]
    for arr in (cache_kidx, cache_k, cache_v):
        in_specs += [page_spec(arr, s, g) for s in range(nb) for g in range(n_pages)]
    in_specs += [const_spec(a) for a in (tab_last, tab_new, cfar, bd)]
    grid_spec = pltpu.PrefetchScalarGridSpec(
        num_scalar_prefetch=1,
        grid=(dec_batch // nb,),
        in_specs=in_specs,
        out_specs=pl.BlockSpec((rows, A_WIDTH), lambda b, pt: (b, 0)),
    )
    npg = nb * n_pages
    args = [p] * 5 + [cache_kidx] * npg + [cache_k] * npg + [cache_v] * npg
    args += [tab_last, tab_new, cfar, bd]
    return pl.pallas_call(
        kern,
        grid_spec=grid_spec,
        out_shape=jax.ShapeDtypeStruct((dec_batch * dec_seq, A_WIDTH), F32),
        compiler_params=_cparams(("arbitrary",)),
        name="dsa_sample",
    )(page_table, *args)


def _mix_kernel(bg_ref, cg_ref, bh_ref, hf_ref, hi_ref, hq_ref, hg_ref, cs_ref, s0_ref,
                cw_ref, lb_ref, ng_ref, bdk_ref, bds_ref, cm_ref, seg_ref,
                o_ref, cso_ref, so_ref,
                uext_ref, st_ref, qi_ref, qa_ref, ka_ref, ks_ref, b_ref, oo_ref, *, chunk, nsb):
    t = pl.program_id(1)
    tt = bg_ref.shape[0] // nsb
    nblk = tt // chunk
    pad = SUBLANES
    lb = lb_ref[...]
    bdk = bdk_ref[...]
    bds = bds_ref[...]
    cmask = cm_ref[...]
    seg = seg_ref[...]
    unroll = math.gcd(nblk, HGRN_UNROLL)

    def tile_rows(x):
        return jnp.broadcast_to(x[None], (C_HEADS,) + x.shape).reshape(C_HEADS * chunk, x.shape[1])

    for s in range(nsb):
        tile = slice(s * tt, (s + 1) * tt)
        uext = uext_ref.at[s]
        st_s = st_ref.at[s]

        @pl.when(t == 0)
        def _():
            uext[pad - (CONV_K - 1):pad, :] = cs_ref[s]
            st_s[...] = jnp.zeros(st_s.shape, F32)
            for h in range(C_HEADS):
                st_s[h * C_DV:(h + 1) * C_DV, h * C_DK:(h + 1) * C_DK] = s0_ref[s, h].T

        u = cg_ref[tile, :] * bh_ref[tile, :]
        uext[pad:pad + tt, :] = u
        y = cw_ref[CONV_K - 1:CONV_K, :] * u
        for j in range(CONV_K - 1):
            y = y + cw_ref[j:j + 1, :] * uext[pad - (CONV_K - 1) + j:pad - (CONV_K - 1) + j + tt, :]
        o_ref[tile, :B_WIDTH] = bg_ref[tile, :] * y
        tail = uext[pad + tt - (CONV_K - 1):pad + tt, :]
        uext[pad - (CONV_K - 1):pad, :] = tail
        cso_ref[s] = tail

        fz = hf_ref[tile, :]
        f = lb + (1.0 - lb) * jax.nn.sigmoid(fz)
        logf = jnp.log(jnp.maximum(f, 1e-30))
        kk = (1.0 - lb) * jax.nn.sigmoid(-fz)
        qz = hq_ref[tile, :]
        qq = qz * jax.nn.sigmoid(qz)
        rowc = lax.broadcasted_iota(I32, (tt, 1), 0) % chunk
        b = logf
        d = 1
        while d < chunk:
            b = b + jnp.where(rowc >= d, pltpu.roll(b, d, 0), 0.0)
            d *= 2
        b3 = b.reshape(nblk, chunk, C_HEADS * C_DK)
        blast3 = b3[:, chunk - 1:chunk, :]
        bmid3 = b3[:, chunk // 2:chunk // 2 + 1, :]
        blast = jnp.broadcast_to(blast3, b3.shape).reshape(tt, C_HEADS * C_DK)
        bmid = jnp.broadcast_to(bmid3, b3.shape).reshape(tt, C_HEADS * C_DK)
        b_ref[tile, :] = b
        qi_ref[tile, :] = qq * jnp.exp(b)
        qa_ref[tile, :] = qq * jnp.exp(jnp.clip(b - bmid, -EXP_CLAMP, EXP_CLAMP))
        ka_ref[tile, :] = kk * jnp.exp(jnp.clip(bmid - b, -EXP_CLAMP, EXP_CLAMP))
        ks_ref[tile, :] = kk * jnp.exp(blast - b)

        def block_body(jo, st, s=s):
            for un in range(unroll):
                r0 = pl.multiple_of(s * tt + (jo * unroll + un) * chunk, chunk)
                rows = pl.ds(r0, chunk)
                v = hi_ref[rows, :]
                o_inter = _nt_dot(qi_ref[rows, :].astype(BF16), st.astype(BF16))
                kabd = (tile_rows(ka_ref[rows, :]) * bdk).astype(BF16)
                amat = _nt_dot(qa_ref[rows, :].astype(BF16), kabd) * cmask
                vbd = (tile_rows(v) * bdk).astype(BF16)
                o_intra = jnp.dot(amat.astype(BF16), vbd, preferred_element_type=F32)
                oo_ref[rows, :] = o_inter + o_intra
                dst = _tn_dot(v.astype(BF16), ks_ref[rows, :].astype(BF16))
                el = jnp.exp(b_ref[pl.ds(r0 + chunk - 1, 1), :])
                st = st * el + dst * bds
            return st

        st_s[...] = lax.fori_loop(0, nblk // unroll, block_body, st_s[...])

        o = oo_ref[tile, :]
        sq = o * o
        hi = sq.astype(BF16)
        lo = (sq - hi.astype(F32)).astype(BF16)
        ms = jnp.dot(hi, seg, preferred_element_type=F32) + jnp.dot(lo, seg, preferred_element_type=F32)
        gz = hg_ref[tile, :]
        o_ref[tile, B_WIDTH:] = o * lax.rsqrt(ms + EPS) * ng_ref[...] * (gz * jax.nn.sigmoid(gz))

        @pl.when(t == pl.num_programs(1) - 1)
        def _():
            stt = st_s[...].T
            for h in range(C_HEADS):
                so_ref[s, h] = stt[h * C_DK:(h + 1) * C_DK, h * C_DV:(h + 1) * C_DV]


def _mix(p, row0, nseq, seq, tt, chunk, nsb, conv_state, s0, cw, lb, ng, consts):
    bdk, bds, cmask, seg = consts
    nt = seq // tt
    assert nseq % nsb == 0 and (nsb == 1 or nt == 1)
    rows = nsb * tt
    rb = row0 // rows

    def pspec(col):
        return pl.BlockSpec((rows, B_WIDTH), lambda s, t: (rb + s * nt + t, col // B_WIDTH))

    def const_spec(a):
        return pl.BlockSpec(a.shape, lambda s, t: (0,) * a.ndim)

    w = C_HEADS * C_DK
    lb = lb.reshape(1, w)
    ng = ng.reshape(1, C_WIDTH)
    return pl.pallas_call(
        functools.partial(_mix_kernel, chunk=chunk, nsb=nsb),
        grid=(nseq // nsb, nt),
        in_specs=[pspec(c) for c in (COL_BG, COL_CG, COL_BH, COL_HF, COL_HI, COL_HQ, COL_HG)] + [
            pl.BlockSpec((nsb, CONV_K - 1, B_WIDTH), lambda s, t: (s, 0, 0)),
            pl.BlockSpec((nsb, C_HEADS, C_DK, C_DV), lambda s, t: (s, 0, 0, 0)),
            const_spec(cw), const_spec(lb), const_spec(ng),
            const_spec(bdk), const_spec(bds), const_spec(cmask), const_spec(seg),
        ],
        out_specs=[
            pl.BlockSpec((rows, B_WIDTH + C_WIDTH), lambda s, t: (s * nt + t, 0)),
            pl.BlockSpec((nsb, CONV_K - 1, B_WIDTH), lambda s, t: (s, 0, 0)),
            pl.BlockSpec((nsb, C_HEADS, C_DK, C_DV), lambda s, t: (s, 0, 0, 0)),
        ],
        out_shape=[
            jax.ShapeDtypeStruct((nseq * seq, B_WIDTH + C_WIDTH), F32),
            jax.ShapeDtypeStruct((nseq, CONV_K - 1, B_WIDTH), F32),
            jax.ShapeDtypeStruct((nseq, C_HEADS, C_DK, C_DV), F32),
        ],
        scratch_shapes=[
            pltpu.VMEM((nsb, tt + SUBLANES, B_WIDTH), F32),
            pltpu.VMEM((nsb, C_WIDTH, w), F32),
            pltpu.VMEM((rows, w), F32),
            pltpu.VMEM((rows, w), F32),
            pltpu.VMEM((rows, w), F32),
            pltpu.VMEM((rows, w), F32),
            pltpu.VMEM((rows, w), F32),
            pltpu.VMEM((rows, C_WIDTH), F32),
        ],
        compiler_params=_cparams(("arbitrary", "arbitrary")),
        name="mix_c%d" % chunk,
    )(p, p, p, p, p, p, p, conv_state, s0, cw, lb, ng, bdk, bds, cmask, seg)


def _mix_consts(chunk):
    w = C_HEADS * C_DK
    rh = jnp.arange(C_HEADS * chunk)[:, None] // chunk
    ch = jnp.arange(w)[None, :] // C_DK
    bdk = (rh == ch).astype(F32)
    hv = jnp.arange(C_WIDTH)[:, None] // C_DV
    bds = (hv == ch).astype(F32)
    tq = jnp.arange(chunk)[:, None]
    sk = jnp.arange(C_HEADS * chunk)[None, :] % chunk
    cmask = (sk <= tq).astype(F32)
    seg = (bds / C_DV).astype(BF16)
    return bdk, bds, cmask, seg


def _t5_bucket(dist):
    dist = jnp.maximum(dist, 0)
    max_exact = N_BUCKETS // 2
    d = jnp.maximum(dist, 1).astype(F32)
    large = max_exact + (jnp.log(d / max_exact) / math.log(MAX_DISTANCE / max_exact)
                         * (N_BUCKETS - max_exact)).astype(I32)
    large = jnp.clip(large, 0, N_BUCKETS - 1)
    return jnp.where(dist < max_exact, dist, large)


def _bias_of_dist(rel_bias, dist):
    bucket = _t5_bucket(dist)
    table = rel_bias.astype(F32) * LOG2E
    expand = (1,) * dist.ndim
    bias = jnp.zeros((A_HEADS,) + dist.shape, F32)
    for b in range(N_BUCKETS):
        bias = jnp.where((bucket == b)[None], table[b].reshape((A_HEADS,) + expand), bias)
    return jnp.where((dist >= 0)[None], bias, NEG_BIG)


def _far_bias(rel_bias, min_dist):
    assert int(16 + math.log(min_dist / 16) / math.log(MAX_DISTANCE / 16) * 16) >= N_BUCKETS - 1
    return rel_bias.astype(F32)[N_BUCKETS - 1] * LOG2E


def _permute_w_in(w_in):
    sizes = (A_WIDTH, A_WIDTH, A_WIDTH, IDX_HEADS * IDX_DIM, IDX_DIM, IDX_HEADS,
             B_WIDTH, B_WIDTH, B_WIDTH, C_HEADS * C_DK, C_WIDTH, C_HEADS * C_DK, C_WIDTH)
    offs = [0]
    for s in sizes:
        offs.append(offs[-1] + s)
    cols = lambda i: w_in[..., offs[i]:offs[i + 1]]
    padw = LANES - IDX_DIM - IDX_HEADS
    pad = jnp.zeros(w_in.shape[:-1] + (padw,), w_in.dtype)
    parts = [cols(i) for i in (0, 1, 2, 3, 6, 7, 8, 9, 10, 11, 12)] + [cols(4), cols(5), pad]
    return jnp.concatenate(parts, axis=-1).astype(BF16)


def kernel(x_prompt, x_sample, cache_k, cache_v, cache_kidx, state_conv, state_hgrn, page_table,
           w_in, w_out, conv_w, hgrn_lb_logits, hgrn_norm_g, rel_bias, norm1_g, norm2_g,
           w_ff1, w_ff2, final_g):
    batch, seq, d_model = x_prompt.shape
    dec_batch, dec_seq, _ = x_sample.shape
    depth = w_in.shape[0]
    n_pool, page = cache_k.shape[1], cache_k.shape[2]
    n_pages = page_table.shape[1]
    past = n_pages * page
    n_p, n_s = batch * seq, dec_batch * dec_seq
    n = n_p + n_s
    assert dec_seq == SUBLANES and seq % (2 * TQ) == 0 and seq % SCORE_CHUNK == 0
    tm = 512 if n % 512 == 0 else 256
    assert n % tm == 0 and n_p % tm == 0

    x = jnp.concatenate([x_prompt.reshape(n_p, d_model), x_sample.reshape(n_s, d_model)], axis=0)
    w_in_p = _permute_w_in(w_in)
    w_out_b = w_out.astype(BF16)
    w_ff1_b = w_ff1.astype(BF16)
    w_ff2_b = w_ff2.astype(BF16)
    sm = jax.nn.softmax(hgrn_lb_logits.astype(F32), axis=0)
    lb_all = jnp.cumsum(sm, axis=0) - sm[0]

    ck = jnp.transpose(cache_k, (0, 1, 3, 4, 2))
    cv = jnp.transpose(cache_v, (0, 1, 3, 4, 2))
    cki = jnp.transpose(cache_kidx, (0, 1, 3, 2))

    qi_ = jnp.arange(TQ)[None, :]
    kj_ = jnp.arange(2 * TQ)[:, None]
    tab_p = jnp.stack([_bias_of_dist(rel_bias, qi_ - kj_), _bias_of_dist(rel_bias, TQ + qi_ - kj_)])
    cfar_p = _far_bias(rel_bias, TQ + 1)
    rows = A_HEADS * dec_seq
    si = jnp.arange(dec_seq)[:, None]
    sj = jnp.arange(LANES)[None, :]
    tab_last = _bias_of_dist(rel_bias, page + si - sj).reshape(rows, LANES)
    d_new = jnp.where(sj < dec_seq, si - sj, -1)
    tab_new = _bias_of_dist(rel_bias, d_new).reshape(rows, LANES)
    cfar_s = jnp.broadcast_to(jnp.repeat(_far_bias(rel_bias, page + 1), dec_seq)[:, None], (rows, LANES))
    bd_s = (jnp.arange(rows)[:, None] // dec_seq == jnp.arange(A_WIDTH)[None, :] // HEAD_DIM).astype(F32)
    tabs_s = (tab_last, tab_new, cfar_s, bd_s)

    chunk_p = math.gcd(seq, HGRN_CHUNK)
    chunk_s = math.gcd(dec_seq, HGRN_CHUNK)
    tt_p = 512 if seq % 512 == 0 else seq
    nsb_s = math.gcd(dec_batch, MIX_SEQS_PER_STEP)
    consts_p = _mix_consts(chunk_p)
    consts_s = _mix_consts(chunk_s)
    conv0 = jnp.zeros((batch, CONV_K - 1, B_WIDTH), F32)
    hgrn0 = jnp.zeros((batch, C_HEADS, C_DK, C_DV), F32)

    kbuf = jnp.zeros((depth, batch, A_WIDTH, seq), F32)
    vbuf = jnp.zeros((depth, batch, A_WIDTH, seq), F32)
    ibuf = jnp.zeros((depth, batch, IDX_DIM, seq), F32)

    ks, vs, kis, cps, sps, css, sss = [], [], [], [], [], [], []
    for l in range(depth):
        p, qt, qit, kb, vt, ikwb, iwt, kbuf, vbuf, ibuf = _inproj(
            x, norm1_g[l], w_in_p[l], tm, l, kbuf, vbuf, ibuf)
        a_p = _dsa_prompt(iwt, qt, qit, kb, ikwb, vt, tab_p, cfar_p, batch, seq)
        a_s = _dsa_sample(p, n_p, cki, ck, cv, l, page_table, tabs_s, dec_batch, dec_seq)
        bc_p, conv_p, s_p = _mix(p, 0, batch, seq, tt_p, chunk_p, 1, conv0, hgrn0,
                                 conv_w[l], lb_all[l], hgrn_norm_g[l], consts_p)
        bc_s, conv_s, s_s = _mix(p, n_p, dec_batch, dec_seq, dec_seq, chunk_s, nsb_s, state_conv[l],
                                 state_hgrn[l], conv_w[l], lb_all[l], hgrn_norm_g[l], consts_s)
        x = _outffn(x, a_p, bc_p, a_s, bc_s, w_out_b[l], norm2_g[l], w_ff1_b[l], w_ff2_b[l], final_g,
                    tm, 1024, l == depth - 1)
        ks.append(p[n_p:, COL_K:COL_K + A_WIDTH])
        vs.append(p[n_p:, COL_V:COL_V + A_WIDTH])
        kis.append(p[n_p:, COL_IKW:COL_IKW + IDX_DIM])
        cps.append(conv_p); sps.append(s_p); css.append(conv_s); sss.append(s_s)

    heads = lambda buf: jnp.transpose(buf.reshape(depth, batch, A_HEADS, HEAD_DIM, seq), (0, 1, 4, 2, 3))
    return (
        x[:n_p].reshape(batch, seq, d_model),
        x[n_p:].reshape(dec_batch, dec_seq, d_model),
        heads(kbuf), heads(vbuf), jnp.transpose(ibuf, (0, 1, 3, 2)),
        jnp.stack(cps), jnp.stack(sps),
        jnp.stack(ks).reshape(depth, dec_batch, dec_seq, A_HEADS, HEAD_DIM),
        jnp.stack(vs).reshape(depth, dec_batch, dec_seq, A_HEADS, HEAD_DIM),
        jnp.stack(kis).reshape(depth, dec_batch, dec_seq, IDX_DIM),
        jnp.stack(css), jnp.stack(sss),
    )
```

```python
import functools
import math

import jax
import jax.numpy as jnp
from jax import lax
from jax.experimental import pallas as pl
from jax.experimental.pallas import tpu as pltpu

F32 = jnp.float32
BF16 = jnp.bfloat16
I32 = jnp.int32

HEAD_DIM = 64
A_HEADS = 8
A_WIDTH = A_HEADS * HEAD_DIM
IDX_HEADS = 8
IDX_DIM = 64
B_WIDTH = 256
C_HEADS = 4
C_DK = 64
C_DV = 64
C_WIDTH = C_HEADS * C_DV
TOPK_MAX = 256
CONV_K = 3
N_BUCKETS = 32
MAX_DISTANCE = 128
HGRN_CHUNK = 16
HGRN_UNROLL = 4
BISECT_GROUP = 4
MIX_SEQS_PER_STEP = 8
SAMPLE_SEQS_PER_STEP = 2
EPS = 1e-6
NEG_BIG = -1e30
LOG2E = 1.4426950408889634
INT_MIN = -(2 ** 31)
INT_MAX = 2 ** 31 - 1
EXP_CLAMP = 80.0

LANES = 128
SUBLANES = 8
VMEM_LIMIT = 56 * 1024 * 1024

COL_Q, COL_K, COL_V, COL_QI = 0, 512, 1024, 1536
COL_BG, COL_CG, COL_BH, COL_HF, COL_HI, COL_HQ, COL_HG = (2048 + 256 * i for i in range(7))
COL_IKW = 3840
P_WIDTH = 3968

TQ = 128
SCORE_CHUNK = 512
ATT_CHUNK = 512


def _cparams(sem):
    return pltpu.CompilerParams(dimension_semantics=sem, vmem_limit_bytes=VMEM_LIMIT)


def _nt_dot(a, b):
    return lax.dot_general(a, b, (((1,), (1,)), ((), ())), preferred_element_type=F32)


def _tn_dot(a, b):
    return lax.dot_general(a, b, (((0,), (0,)), ((), ())), preferred_element_type=F32)


def _float_key(x):
    bits = pltpu.bitcast(x, I32)
    return bits ^ ((bits >> 31) & INT_MAX)


def _inproj_kernel(x_ref, g_ref, w_ref, ko_in, vo_in, io_in,
                   p_ref, qt_ref, qit_ref, kb_ref, vt_ref, ikw_ref, iwt_ref, ko_ref, vo_ref, io_ref,
                   *, n_prompt_blocks):
    del ko_in, vo_in, io_in
    x = x_ref[...]
    ms = jnp.mean(x * x, axis=-1, keepdims=True)
    xn = (x * lax.rsqrt(ms + EPS) * g_ref[...]).astype(BF16)
    res = jnp.dot(xn, w_ref[...], preferred_element_type=F32)
    p_ref[...] = res
    q_scale = (HEAD_DIM ** -0.5) * LOG2E
    w_scale = (IDX_HEADS ** -0.5) * (IDX_DIM ** -0.5)
    qt_ref[...] = (res[:, COL_Q:COL_Q + A_WIDTH] * q_scale).T.astype(BF16)
    qit_ref[...] = res[:, COL_QI:COL_QI + A_WIDTH].T.astype(BF16)
    k = res[:, COL_K:COL_K + A_WIDTH]
    kb_ref[...] = k.astype(BF16)
    vt = res[:, COL_V:COL_V + A_WIDTH].T
    vt_ref[...] = vt.astype(BF16)
    ikw = res[:, COL_IKW:COL_IKW + LANES]
    ikw_ref[...] = ikw.astype(BF16)
    ikwt = ikw.T
    iwt_ref[...] = ikwt[IDX_DIM:IDX_DIM + IDX_HEADS] * w_scale

    @pl.when(pl.program_id(0) < n_prompt_blocks)
    def _():
        ko_ref[...] = k.T
        vo_ref[...] = vt
        io_ref[...] = ikwt[:IDX_DIM]


def _inproj(x, g, w, tm, layer, kbuf, vbuf, ibuf):
    n, d = x.shape
    batch, seq = kbuf.shape[1], kbuf.shape[3]
    tpb = seq // tm
    npb = batch * tpb

    def slab_spec(width):
        def index(i):
            ii = jnp.minimum(i, npb - 1)
            return (layer, ii // tpb, 0, ii % tpb)
        return pl.BlockSpec((None, None, width, tm), index)

    any_spec = pl.BlockSpec(memory_space=pl.ANY)
    return pl.pallas_call(
        functools.partial(_inproj_kernel, n_prompt_blocks=npb),
        grid=(n // tm,),
        in_specs=[
            pl.BlockSpec((tm, d), lambda i: (i, 0)),
            pl.BlockSpec((1, d), lambda i: (0, 0)),
            pl.BlockSpec((d, P_WIDTH), lambda i: (0, 0)),
            any_spec, any_spec, any_spec,
        ],
        input_output_aliases={3: 7, 4: 8, 5: 9},
        out_specs=[
            pl.BlockSpec((tm, P_WIDTH), lambda i: (i, 0)),
            pl.BlockSpec((A_WIDTH, tm), lambda i: (0, i)),
            pl.BlockSpec((A_WIDTH, tm), lambda i: (0, i)),
            pl.BlockSpec((tm, A_WIDTH), lambda i: (i, 0)),
            pl.BlockSpec((A_WIDTH, tm), lambda i: (0, i)),
            pl.BlockSpec((tm, LANES), lambda i: (i, 0)),
            pl.BlockSpec((IDX_HEADS, tm), lambda i: (0, i)),
            slab_spec(A_WIDTH), slab_spec(A_WIDTH), slab_spec(IDX_DIM),
        ],
        out_shape=[
            jax.ShapeDtypeStruct((n, P_WIDTH), F32),
            jax.ShapeDtypeStruct((A_WIDTH, n), BF16),
            jax.ShapeDtypeStruct((A_WIDTH, n), BF16),
            jax.ShapeDtypeStruct((n, A_WIDTH), BF16),
            jax.ShapeDtypeStruct((A_WIDTH, n), BF16),
            jax.ShapeDtypeStruct((n, LANES), BF16),
            jax.ShapeDtypeStruct((IDX_HEADS, n), F32),
            jax.ShapeDtypeStruct(kbuf.shape, F32),
            jax.ShapeDtypeStruct(vbuf.shape, F32),
            jax.ShapeDtypeStruct(ibuf.shape, F32),
        ],
        compiler_params=_cparams(("arbitrary",)),
        name="inproj",
    )(x, g.reshape(1, d), w, kbuf, vbuf, ibuf)


def _outffn_kernel(x_ref, ap_ref, bcp_ref, as_ref, bcs_ref, wo_ref, g2_ref, w1_ref, w2_ref, gf_ref,
                   o_ref, acc_ref, hn_ref, *, final, n_prompt_blocks):
    i = pl.program_id(0)
    j = pl.program_id(1)

    def mix_in(a_ref, bc_ref):
        x1 = x_ref[...]
        x1 = x1 + jnp.dot(a_ref[...].astype(BF16), wo_ref[:A_WIDTH, :], preferred_element_type=F32)
        x1 = x1 + jnp.dot(bc_ref[...].astype(BF16), wo_ref[A_WIDTH:, :], preferred_element_type=F32)
        acc_ref[...] = x1
        ms = jnp.mean(x1 * x1, axis=-1, keepdims=True)
        hn_ref[...] = (x1 * lax.rsqrt(ms + EPS) * g2_ref[...]).astype(BF16)

    @pl.when(jnp.logical_and(j == 0, i < n_prompt_blocks))
    def _():
        mix_in(ap_ref, bcp_ref)

    @pl.when(jnp.logical_and(j == 0, i >= n_prompt_blocks))
    def _():
        mix_in(as_ref, bcs_ref)

    h = jnp.dot(hn_ref[...], w1_ref[...], preferred_element_type=F32)
    h = jnp.maximum(h, 0.0)
    h = (h * h).astype(BF16)
    acc_ref[...] += jnp.dot(h, w2_ref[...], preferred_element_type=F32)

    @pl.when(j == pl.num_programs(1) - 1)
    def _():
        y = acc_ref[...]
        if final:
            ms = jnp.mean(y * y, axis=-1, keepdims=True)
            y = y * lax.rsqrt(ms + EPS) * gf_ref[...]
        o_ref[...] = y


def _outffn(x, a_p, bc_p, a_s, bc_s, wo, g2, w1, w2, gf, tm, tf, final):
    n, d = x.shape
    dff = w1.shape[1]
    npb = a_p.shape[0] // tm
    prompt_rows = lambda i, j: (jnp.minimum(i, npb - 1), 0)
    sample_rows = lambda i, j: (jnp.maximum(i - npb, 0), 0)
    return pl.pallas_call(
        functools.partial(_outffn_kernel, final=final, n_prompt_blocks=npb),
        grid=(n // tm, dff // tf),
        in_specs=[
            pl.BlockSpec((tm, d), lambda i, j: (i, 0)),
            pl.BlockSpec((tm, A_WIDTH), prompt_rows),
            pl.BlockSpec((tm, B_WIDTH + C_WIDTH), prompt_rows),
            pl.BlockSpec((tm, A_WIDTH), sample_rows),
            pl.BlockSpec((tm, B_WIDTH + C_WIDTH), sample_rows),
            pl.BlockSpec(wo.shape, lambda i, j: (0, 0)),
            pl.BlockSpec((1, d), lambda i, j: (0, 0)),
            pl.BlockSpec((d, tf), lambda i, j: (0, j)),
            pl.BlockSpec((tf, d), lambda i, j: (j, 0)),
            pl.BlockSpec((1, d), lambda i, j: (0, 0)),
        ],
        out_specs=pl.BlockSpec((tm, d), lambda i, j: (i, 0)),
        out_shape=jax.ShapeDtypeStruct((n, d), F32),
        scratch_shapes=[pltpu.VMEM((tm, d), F32), pltpu.VMEM((tm, d), BF16)],
        compiler_params=_cparams(("arbitrary", "arbitrary")),
        name="outffn_final" if final else "outffn",
    )(x, a_p, bc_p, a_s, bc_s, wo, g2.reshape(1, d), w1, w2, gf.reshape(1, d))


def _topk_threshold(count_ge, count_eq_before, like, topk, n_pos):
    kf = float(topk)
    ans = _bisect_keys(count_ge, like, topk)
    c_ans = count_ge(ans)
    pos_bits = max(1, (n_pos - 1).bit_length())

    def tie_cut(_):
        need = kf - count_ge(ans + 1)

        def jbody(i, xcut):
            cand = xcut + (jnp.int32(1) << (pos_bits - 1 - i))
            return jnp.where(count_eq_before(ans, cand) < need, cand, xcut)

        cut = lax.fori_loop(0, pos_bits, jbody, jnp.zeros_like(like))
        return jnp.where(c_ans > kf, cut, n_pos)

    jcut = lax.cond(jnp.max(c_ans) > kf, tie_cut, lambda _: jnp.full_like(like, n_pos), 0)
    return ans, jcut


def _bisect_keys(count_ge, like, topk):
    kf = float(topk)
    c0 = count_ge(jnp.zeros_like(like))
    c1 = count_ge(jnp.ones_like(like))
    ans0 = jnp.where(c0 >= kf, 0, INT_MIN).astype(I32)
    open0 = jnp.where(jnp.logical_or(c0 == kf, jnp.logical_and(c1 < kf, c0 >= kf)), 0.0, 1.0)

    def step(i, ans, opn):
        cand = ans + (jnp.int32(1) << (30 - i))
        c = count_ge(cand)
        take = jnp.logical_and(c >= kf, opn > 0.0)
        return jnp.where(take, cand, ans), jnp.where(c == kf, 0.0, opn)

    def group(st):
        g, ans, opn = st
        for u in range(BISECT_GROUP):
            ans, opn = step(g * BISECT_GROUP + u, ans, opn)
        return g + 1, ans, opn

    def pending(st):
        g, _, opn = st
        return jnp.logical_and(g < 31 // BISECT_GROUP, jnp.max(opn) > 0.0)

    g, ans, opn = lax.while_loop(pending, group, (jnp.int32(0), ans0, open0))

    def tail(_):
        return lax.fori_loop(g * BISECT_GROUP, 31, lambda i, st: step(i, *st), (ans, opn))[0]

    return lax.cond(jnp.max(opn) > 0.0, tail, lambda _: ans, 0)


def _dsa_prompt_kernel(cfar_ref, iwt_ref, qt_ref, qit_ref, kb_ref, ikw_ref, vt_ref, tab_ref, tril_ref,
                       o_ref, sc_ref, qbd_ref, qibd_ref, acc0_ref, acc1_ref, acc2_ref, acc3_ref,
                       ans_ref, *, seq, topk):
    qb = pl.program_id(1)
    tq = TQ
    qs = qb * tq
    npair = A_HEADS // 2

    zq = jnp.zeros((HEAD_DIM, tq), BF16)
    for j in range(npair):
        top = qt_ref[(2 * j) * HEAD_DIM:(2 * j + 1) * HEAD_DIM, :]
        bot = qt_ref[(2 * j + 1) * HEAD_DIM:(2 * j + 2) * HEAD_DIM, :]
        qbd_ref[j] = jnp.concatenate([jnp.concatenate([top, zq], axis=1),
                                      jnp.concatenate([zq, bot], axis=1)], axis=0)
        itop = qit_ref[(2 * j) * IDX_DIM:(2 * j + 1) * IDX_DIM, :]
        ibot = qit_ref[(2 * j + 1) * IDX_DIM:(2 * j + 2) * IDX_DIM, :]
        qibd_ref[j] = jnp.concatenate([jnp.concatenate([itop, ibot], axis=1),
                                       jnp.zeros((LANES - IDX_DIM, 2 * tq), BF16)], axis=0)

    wrow = iwt_ref[...]
    tpos = qs + lax.broadcasted_iota(I32, (1, tq), 1)

    def fold(x, op):
        while x.shape[0] > SUBLANES:
            half = x.shape[0] // 2
            x = op(x[:half], x[half:])
        return x

    n_sc = (qs + tq + SCORE_CHUNK - 1) // SCORE_CHUNK

    def score_body(c, carry):
        c0 = pl.multiple_of(c * SCORE_CHUNK, SCORE_CHUNK)
        ikc = ikw_ref[pl.ds(c0, SCORE_CHUNK), :]
        score = jnp.zeros((SCORE_CHUNK, tq), F32)
        for j in range(npair):
            s = jnp.dot(ikc, qibd_ref[j], preferred_element_type=F32)
            score = score + jnp.maximum(s[:, :tq], 0.0) * wrow[2 * j:2 * j + 1, :]
            score = score + jnp.maximum(s[:, tq:], 0.0) * wrow[2 * j + 1:2 * j + 2, :]
        pos = c0 + lax.broadcasted_iota(I32, (SCORE_CHUNK, tq), 0)
        sc_ref[pl.ds(c0, SCORE_CHUNK), :] = _float_key(jnp.where(pos <= tpos, score, NEG_BIG))
        return carry

    lax.fori_loop(0, n_sc, score_body, 0)

    def count_ge(cand):
        def body(c, acc):
            c0 = pl.multiple_of(c * SCORE_CHUNK, SCORE_CHUNK)
            keys = sc_ref[pl.ds(c0, SCORE_CHUNK), :]
            return acc + fold(jnp.where(keys >= cand, 1.0, 0.0), jnp.add)
        acc = lax.fori_loop(0, n_sc, body, jnp.zeros((SUBLANES, tq), F32))
        return jnp.sum(acc, axis=0, keepdims=True)

    ans_ref[...] = jnp.full((1, tq), INT_MIN + 1, I32)

    @pl.when(qs + tq > topk)
    def _():
        kf = float(topk)
        ans = _bisect_keys(count_ge, jnp.zeros((1, tq), I32), topk)
        ans_ref[...] = ans

        @pl.when(jnp.max(count_ge(ans)) > kf)
        def _():
            need = kf - count_ge(ans + 1)

            def demote(c, base):
                c0 = pl.multiple_of(c * SCORE_CHUNK, SCORE_CHUNK)
                keys = sc_ref[pl.ds(c0, SCORE_CHUNK), :]
                tied = keys == ans
                rank = base + jnp.dot(tril_ref[...], jnp.where(tied, 1.0, 0.0).astype(BF16),
                                      preferred_element_type=F32)
                sc_ref[pl.ds(c0, SCORE_CHUNK), :] = jnp.where(
                    tied, jnp.where(rank > need, ans - 1, keys), keys)
                return rank[SCORE_CHUNK - 1:SCORE_CHUNK, :]

            lax.fori_loop(0, n_sc, demote, jnp.zeros((1, tq), F32))

    acc_refs = (acc0_ref, acc1_ref, acc2_ref, acc3_ref)
    for r in acc_refs:
        r[...] = jnp.zeros(r.shape, F32)
    ans = ans_ref[...]

    def attend_chunk(c0, width, limit, tsel, ms, lss):
        keys = sc_ref[pl.ds(c0, width), :]
        pos = c0 + lax.broadcasted_iota(I32, (width, tq), 0)
        sel = keys >= jnp.where(pos < limit, ans, INT_MAX)
        lgs = [jnp.dot(kb_ref[pl.ds(c0, width), j * LANES:(j + 1) * LANES], qbd_ref[j],
                       preferred_element_type=F32) for j in range(npair)]
        ms_new, lss_new = [], []
        for j in range(npair):
            ps, alphas = [], []
            for half in range(2):
                h = 2 * j + half
                lh = lgs[j][:, half * tq:(half + 1) * tq]
                if tsel is None:
                    cb = cfar_ref[h]
                else:
                    lh = lh + tab_ref[tsel, h]
                    cb = 0.0
                lh = jnp.where(sel, lh, NEG_BIG)
                cmax = jnp.max(fold(lh, jnp.maximum), axis=0, keepdims=True)
                m_new = jnp.maximum(ms[h], cmax + cb)
                alpha = jnp.exp2(ms[h] - m_new)
                p = jnp.exp2(lh - (m_new - cb))
                ms_new.append(m_new)
                lss_new.append(alpha * lss[h] + fold(p, jnp.add))
                ps.append(p.astype(BF16))
                alphas.append(alpha)
            vt = vt_ref[j * LANES:(j + 1) * LANES, pl.ds(c0, width)]
            pv = jnp.dot(vt, jnp.concatenate(ps, axis=1), preferred_element_type=F32)
            acc_refs[j][...] = acc_refs[j][...] * jnp.concatenate(alphas, axis=1) + pv
        return tuple(ms_new), tuple(lss_new)

    near0 = jnp.maximum(qb - 1, 0) * tq

    def far_body(c, carry):
        return attend_chunk(pl.multiple_of(c * ATT_CHUNK, ATT_CHUNK), ATT_CHUNK, near0, None, *carry)

    init = (tuple(jnp.full((1, tq), NEG_BIG, F32) for _ in range(A_HEADS)),
            tuple(jnp.zeros((SUBLANES, tq), F32) for _ in range(A_HEADS)))
    carry = lax.fori_loop(0, (near0 + ATT_CHUNK - 1) // ATT_CHUNK, far_body, init)
    _, lss = attend_chunk(pl.multiple_of(near0, tq), 2 * tq, seq, jnp.minimum(qb, 1), *carry)

    for j in range(npair):
        outs = []
        for half in range(2):
            inv = 1.0 / jnp.sum(lss[2 * j + half], axis=0, keepdims=True)
            blk = acc_refs[j][half * HEAD_DIM:(half + 1) * HEAD_DIM, half * tq:(half + 1) * tq]
            outs.append((blk * inv).T)
        o_ref[:, j * LANES:(j + 1) * LANES] = jnp.concatenate(outs, axis=1)


def _dsa_prompt(iwt, qt, qit, kb, ikwb, vt, tab, cfar, batch, seq):
    tq = TQ
    tril = (jnp.arange(SCORE_CHUNK)[None, :] <= jnp.arange(SCORE_CHUNK)[:, None]).astype(BF16)
    nq = seq // tq
    topk = min(TOPK_MAX, seq // 4)
    kern = functools.partial(_dsa_prompt_kernel, seq=seq, topk=topk)
    return pl.pallas_call(
        kern,
        grid=(batch, nq),
        in_specs=[
            pl.BlockSpec(memory_space=pltpu.SMEM),
            pl.BlockSpec((IDX_HEADS, tq), lambda b, i: (0, b * nq + i)),
            pl.BlockSpec((A_WIDTH, tq), lambda b, i: (0, b * nq + i)),
            pl.BlockSpec((A_WIDTH, tq), lambda b, i: (0, b * nq + i)),
            pl.BlockSpec((seq, A_WIDTH), lambda b, i: (b, 0)),
            pl.BlockSpec((seq, LANES), lambda b, i: (b, 0)),
            pl.BlockSpec((A_WIDTH, seq), lambda b, i: (0, b)),
            pl.BlockSpec(tab.shape, lambda b, i: (0, 0, 0, 0)),
            pl.BlockSpec(tril.shape, lambda b, i: (0, 0)),
        ],
        out_specs=pl.BlockSpec((tq, A_WIDTH), lambda b, i: (b * nq + i, 0)),
        out_shape=jax.ShapeDtypeStruct((batch * seq, A_WIDTH), F32),
        scratch_shapes=[
            pltpu.VMEM((seq, tq), I32),
            pltpu.VMEM((A_HEADS // 2, 2 * HEAD_DIM, 2 * tq), BF16),
            pltpu.VMEM((IDX_HEADS // 2, LANES, 2 * tq), BF16),
            pltpu.VMEM((2 * HEAD_DIM, 2 * tq), F32),
            pltpu.VMEM((2 * HEAD_DIM, 2 * tq), F32),
            pltpu.VMEM((2 * HEAD_DIM, 2 * tq), F32),
            pltpu.VMEM((2 * HEAD_DIM, 2 * tq), F32),
            pltpu.VMEM((1, tq), I32),
        ],
        compiler_params=_cparams(("arbitrary", "arbitrary")),
        name="dsa_prompt",
    )(cfar, iwt, qt, qit, kb, ikwb, vt, tab, tril)


def _dsa_sample_kernel(pt_ref, q_ref, kn_ref, vn_ref, qi_ref, ikw_ref, *rest, n_pages, page, topk, nb):
    npg = nb * n_pages
    kidx_refs, k_refs, v_refs = rest[:npg], rest[npg:2 * npg], rest[2 * npg:3 * npg]
    tab_last_ref, tab_new_ref, cfar_ref, bd_ref, o_ref = rest[3 * npg:]
    ds = q_ref.shape[0] // nb
    rows = A_HEADS * ds
    past = n_pages * page
    width = past + LANES
    w_scale = (IDX_HEADS ** -0.5) * (IDX_DIM ** -0.5)
    q_scale = (HEAD_DIM ** -0.5) * LOG2E

    def tile_heads(x):
        return jnp.broadcast_to(x[None], (A_HEADS,) + x.shape).reshape(rows, x.shape[1])

    def pad_rows(x):
        return jnp.concatenate([x, jnp.zeros((LANES - ds, x.shape[1]), x.dtype)], axis=0)

    qrow = lax.broadcasted_iota(I32, (ds, LANES), 0)
    kcol = lax.broadcasted_iota(I32, (ds, LANES), 1)

    def seq_scores(s):
        r = slice(s * ds, (s + 1) * ds)
        qi = qi_ref[r, :]
        q2 = jnp.concatenate([qi[:, h * IDX_DIM:(h + 1) * IDX_DIM] for h in range(IDX_HEADS)],
                             axis=0).astype(BF16)
        wi = ikw_ref[r, IDX_DIM:IDX_DIM + IDX_HEADS] * w_scale
        wcols = [jnp.broadcast_to(wi[:, h:h + 1], (ds, LANES)) for h in range(IDX_HEADS)]

        def head_sum(sc):
            out = jnp.zeros((ds, LANES), F32)
            for h in range(IDX_HEADS):
                out = out + jnp.maximum(sc[h * ds:(h + 1) * ds], 0.0) * wcols[h]
            return out

        scores = [head_sum(jnp.dot(q2, kidx_refs[s * n_pages + p][0].astype(BF16),
                                   preferred_element_type=F32)) for p in range(n_pages)]
        ik_new = pad_rows(ikw_ref[r, :IDX_DIM]).astype(BF16)
        s_new = jnp.where(kcol <= qrow, head_sum(_nt_dot(q2, ik_new)), NEG_BIG)
        return jnp.concatenate(scores + [s_new], axis=1)

    keys = _float_key(jnp.concatenate([seq_scores(s) for s in range(nb)], axis=0))
    pos = lax.broadcasted_iota(I32, (1, width), 1)

    def count_ge(cand):
        return jnp.sum(jnp.where(keys >= cand, 1.0, 0.0), axis=1, keepdims=True)

    def count_eq_before(a, xcut):
        hit = jnp.where(keys == a, jnp.where(pos < xcut, 1.0, 0.0), 0.0)
        return jnp.sum(hit, axis=1, keepdims=True)

    ans, jcut = _topk_threshold(count_ge, count_eq_before, jnp.zeros((nb * ds, 1), I32), topk, width)
    thr = jnp.where(pos <= jcut, ans - 1, ans)
    addmask_all = jnp.where(keys > thr, 0.0, NEG_BIG)

    bd = bd_ref[...]
    for s in range(nb):
        r = slice(s * ds, (s + 1) * ds)
        addmask = addmask_all[r]
        qbd = (tile_heads(q_ref[r, :] * q_scale) * bd).astype(BF16)
        logits = []
        for p in range(n_pages):
            kt = k_refs[s * n_pages + p][0].reshape(A_WIDTH, page).astype(BF16)
            lg = jnp.dot(qbd, kt, preferred_element_type=F32)
            lg = lg + (tab_last_ref[...] if p == n_pages - 1 else cfar_ref[...])
            logits.append(lg + tile_heads(addmask[:, p * page:(p + 1) * page]))
        lg = _nt_dot(qbd, pad_rows(kn_ref[r, :]).astype(BF16)) + tab_new_ref[...]
        logits.append(lg + tile_heads(addmask[:, past:]))
        m = functools.reduce(jnp.maximum, [jnp.max(l, axis=1, keepdims=True) for l in logits])
        acc = jnp.zeros((rows, A_WIDTH), F32)
        lsum = jnp.zeros((rows, 1), F32)
        for p in range(n_pages + 1):
            pr = jnp.exp2(logits[p] - m)
            lsum = lsum + jnp.sum(pr, axis=1, keepdims=True)
            if p < n_pages:
                vt = v_refs[s * n_pages + p][0].reshape(A_WIDTH, page).astype(BF16)
                acc = acc + _nt_dot(pr.astype(BF16), vt)
            else:
                acc = acc + jnp.dot(pr.astype(BF16), pad_rows(vn_ref[r, :]).astype(BF16),
                                    preferred_element_type=F32)
        acc = acc * (1.0 / lsum) * bd
        out = acc[0:ds]
        for h in range(1, A_HEADS):
            out = out + acc[h * ds:(h + 1) * ds]
        o_ref[r, :] = out


def _dsa_sample(p, row0, cache_kidx, cache_k, cache_v, layer, page_table, tabs, dec_batch, dec_seq):
    n_pages = page_table.shape[1]
    page = cache_k.shape[-1]
    past = n_pages * page
    topk = min(TOPK_MAX, (past + dec_seq) // 4)
    tab_last, tab_new, cfar, bd = tabs
    nb = math.gcd(dec_batch, SAMPLE_SEQS_PER_STEP)
    rows = nb * dec_seq
    rb = row0 // rows
    kern = functools.partial(_dsa_sample_kernel, n_pages=n_pages, page=page, topk=topk, nb=nb)

    def pspec(col, w):
        return pl.BlockSpec((rows, w), lambda b, pt: (rb + b, col // w))

    def page_spec(arr, s, pg):
        blk = (None, 1) + arr.shape[2:]
        zeros = (0,) * (arr.ndim - 2)
        return pl.BlockSpec(blk, lambda b, pt: (layer, pt[b * nb + s, pg]) + zeros)

    def const_spec(a):
        return pl.BlockSpec(a.shape, lambda b, pt: (0,) * a.ndim)

    in_specs = [pspec(COL_Q, A_WIDTH), pspec(COL_K, A_WIDTH), pspec(COL_V, A_WIDTH),
                pspec(COL_QI, A_WIDTH), pspec(COL_IKW, LANES)]
    for arr in (cache_kidx, cache_k, cache_v):
        in_specs += [page_spec(arr, s, g) for s in range(nb) for g in range(n_pages)]
    in_specs += [const_spec(a) for a in (tab_last, tab_new, cfar, bd)]
    grid_spec = pltpu.PrefetchScalarGridSpec(
        num_scalar_prefetch=1,
        grid=(dec_batch // nb,),
        in_specs=in_specs,
        out_specs=pl.BlockSpec((rows, A_WIDTH), lambda b, pt: (b, 0)),
    )
    npg = nb * n_pages
    args = [p] * 5 + [cache_kidx] * npg + [cache_k] * npg + [cache_v] * npg
    args += [tab_last, tab_new, cfar, bd]
    return pl.pallas_call(
        kern,
        grid_spec=grid_spec,
        out_shape=jax.ShapeDtypeStruct((dec_batch * dec_seq, A_WIDTH), F32),
        compiler_params=_cparams(("arbitrary",)),
        name="dsa_sample",
    )(page_table, *args)


def _mix_kernel(bg_ref, cg_ref, bh_ref, hf_ref, hi_ref, hq_ref, hg_ref, cs_ref, s0_ref,
                cw_ref, lb_ref, ng_ref, bdk_ref, bds_ref, cm_ref, seg_ref,
                o_ref, cso_ref, so_ref,
                uext_ref, st_ref, qi_ref, qa_ref, ka_ref, ks_ref, b_ref, oo_ref, *, chunk, nsb):
    t = pl.program_id(1)
    tt = bg_ref.shape[0] // nsb
    nblk = tt // chunk
    pad = SUBLANES
    lb = lb_ref[...]
    bdk = bdk_ref[...]
    bds = bds_ref[...]
    cmask = cm_ref[...]
    seg = seg_ref[...]
    unroll = math.gcd(nblk, HGRN_UNROLL)

    def tile_rows(x):
        return jnp.broadcast_to(x[None], (C_HEADS,) + x.shape).reshape(C_HEADS * chunk, x.shape[1])

    for s in range(nsb):
        tile = slice(s * tt, (s + 1) * tt)
        uext = uext_ref.at[s]
        st_s = st_ref.at[s]

        @pl.when(t == 0)
        def _():
            uext[pad - (CONV_K - 1):pad, :] = cs_ref[s]
            st_s[...] = jnp.zeros(st_s.shape, F32)
            for h in range(C_HEADS):
                st_s[h * C_DV:(h + 1) * C_DV, h * C_DK:(h + 1) * C_DK] = s0_ref[s, h].T

        u = cg_ref[tile, :] * bh_ref[tile, :]
        uext[pad:pad + tt, :] = u
        y = cw_ref[CONV_K - 1:CONV_K, :] * u
        for j in range(CONV_K - 1):
            y = y + cw_ref[j:j + 1, :] * uext[pad - (CONV_K - 1) + j:pad - (CONV_K - 1) + j + tt, :]
        o_ref[tile, :B_WIDTH] = bg_ref[tile, :] * y
        tail = uext[pad + tt - (CONV_K - 1):pad + tt, :]
        uext[pad - (CONV_K - 1):pad, :] = tail
        cso_ref[s] = tail

        fz = hf_ref[tile, :]
        f = lb + (1.0 - lb) * jax.nn.sigmoid(fz)
        logf = jnp.log(jnp.maximum(f, 1e-30))
        kk = (1.0 - lb) * jax.nn.sigmoid(-fz)
        qz = hq_ref[tile, :]
        qq = qz * jax.nn.sigmoid(qz)
        rowc = lax.broadcasted_iota(I32, (tt, 1), 0) % chunk
        b = logf
        d = 1
        while d < chunk:
            b = b + jnp.where(rowc >= d, pltpu.roll(b, d, 0), 0.0)
            d *= 2
        b3 = b.reshape(nblk, chunk, C_HEADS * C_DK)
        blast3 = b3[:, chunk - 1:chunk, :]
        bmid3 = b3[:, chunk // 2:chunk // 2 + 1, :]
        blast = jnp.broadcast_to(blast3, b3.shape).reshape(tt, C_HEADS * C_DK)
        bmid = jnp.broadcast_to(bmid3, b3.shape).reshape(tt, C_HEADS * C_DK)
        b_ref[tile, :] = b
        qi_ref[tile, :] = qq * jnp.exp(b)
        qa_ref[tile, :] = qq * jnp.exp(jnp.clip(b - bmid, -EXP_CLAMP, EXP_CLAMP))
        ka_ref[tile, :] = kk * jnp.exp(jnp.clip(bmid - b, -EXP_CLAMP, EXP_CLAMP))
        ks_ref[tile, :] = kk * jnp.exp(blast - b)

        def block_body(jo, st, s=s):
            for un in range(unroll):
                r0 = pl.multiple_of(s * tt + (jo * unroll + un) * chunk, chunk)
                rows = pl.ds(r0, chunk)
                v = hi_ref[rows, :]
                o_inter = _nt_dot(qi_ref[rows, :].astype(BF16), st.astype(BF16))
                kabd = (tile_rows(ka_ref[rows, :]) * bdk).astype(BF16)
                amat = _nt_dot(qa_ref[rows, :].astype(BF16), kabd) * cmask
                vbd = (tile_rows(v) * bdk).astype(BF16)
                o_intra = jnp.dot(amat.astype(BF16), vbd, preferred_element_type=F32)
                oo_ref[rows, :] = o_inter + o_intra
                dst = _tn_dot(v.astype(BF16), ks_ref[rows, :].astype(BF16))
                el = jnp.exp(b_ref[pl.ds(r0 + chunk - 1, 1), :])
                st = st * el + dst * bds
            return st

        st_s[...] = lax.fori_loop(0, nblk // unroll, block_body, st_s[...])

        o = oo_ref[tile, :]
        sq = o * o
        hi = sq.astype(BF16)
        lo = (sq - hi.astype(F32)).astype(BF16)
        ms = jnp.dot(hi, seg, preferred_element_type=F32) + jnp.dot(lo, seg, preferred_element_type=F32)
        gz = hg_ref[tile, :]
        o_ref[tile, B_WIDTH:] = o * lax.rsqrt(ms + EPS) * ng_ref[...] * (gz * jax.nn.sigmoid(gz))

        @pl.when(t == pl.num_programs(1) - 1)
        def _():
            stt = st_s[...].T
            for h in range(C_HEADS):
                so_ref[s, h] = stt[h * C_DK:(h + 1) * C_DK, h * C_DV:(h + 1) * C_DV]


def _mix(p, row0, nseq, seq, tt, chunk, nsb, conv_state, s0, cw, lb, ng, consts):
    bdk, bds, cmask, seg = consts
    nt = seq // tt
    assert nseq % nsb == 0 and (nsb == 1 or nt == 1)
    rows = nsb * tt
    rb = row0 // rows

    def pspec(col):
        return pl.BlockSpec((rows, B_WIDTH), lambda s, t: (rb + s * nt + t, col // B_WIDTH))

    def const_spec(a):
        return pl.BlockSpec(a.shape, lambda s, t: (0,) * a.ndim)

    w = C_HEADS * C_DK
    lb = lb.reshape(1, w)
    ng = ng.reshape(1, C_WIDTH)
    return pl.pallas_call(
        functools.partial(_mix_kernel, chunk=chunk, nsb=nsb),
        grid=(nseq // nsb, nt),
        in_specs=[pspec(c) for c in (COL_BG, COL_CG, COL_BH, COL_HF, COL_HI, COL_HQ, COL_HG)] + [
            pl.BlockSpec((nsb, CONV_K - 1, B_WIDTH), lambda s, t: (s, 0, 0)),
            pl.BlockSpec((nsb, C_HEADS, C_DK, C_DV), lambda s, t: (s, 0, 0, 0)),
            const_spec(cw), const_spec(lb), const_spec(ng),
            const_spec(bdk), const_spec(bds), const_spec(cmask), const_spec(seg),
        ],
        out_specs=[
            pl.BlockSpec((rows, B_WIDTH + C_WIDTH), lambda s, t: (s * nt + t, 0)),
            pl.BlockSpec((nsb, CONV_K - 1, B_WIDTH), lambda s, t: (s, 0, 0)),
            pl.BlockSpec((nsb, C_HEADS, C_DK, C_DV), lambda s, t: (s, 0, 0, 0)),
        ],
        out_shape=[
            jax.ShapeDtypeStruct((nseq * seq, B_WIDTH + C_WIDTH), F32),
            jax.ShapeDtypeStruct((nseq, CONV_K - 1, B_WIDTH), F32),
            jax.ShapeDtypeStruct((nseq, C_HEADS, C_DK, C_DV), F32),
        ],
        scratch_shapes=[
            pltpu.VMEM((nsb, tt + SUBLANES, B_WIDTH), F32),
            pltpu.VMEM((nsb, C_WIDTH, w), F32),
            pltpu.VMEM((rows, w), F32),
            pltpu.VMEM((rows, w), F32),
            pltpu.VMEM((rows, w), F32),
            pltpu.VMEM((rows, w), F32),
            pltpu.VMEM((rows, w), F32),
            pltpu.VMEM((rows, C_WIDTH), F32),
        ],
        compiler_params=_cparams(("arbitrary", "arbitrary")),
        name="mix_c%d" % chunk,
    )(p, p, p, p, p, p, p, conv_state, s0, cw, lb, ng, bdk, bds, cmask, seg)


def _mix_consts(chunk):
    w = C_HEADS * C_DK
    rh = jnp.arange(C_HEADS * chunk)[:, None] // chunk
    ch = jnp.arange(w)[None, :] // C_DK
    bdk = (rh == ch).astype(F32)
    hv = jnp.arange(C_WIDTH)[:, None] // C_DV
    bds = (hv == ch).astype(F32)
    tq = jnp.arange(chunk)[:, None]
    sk = jnp.arange(C_HEADS * chunk)[None, :] % chunk
    cmask = (sk <= tq).astype(F32)
    seg = (bds / C_DV).astype(BF16)
    return bdk, bds, cmask, seg


def _t5_bucket(dist):
    dist = jnp.maximum(dist, 0)
    max_exact = N_BUCKETS // 2
    d = jnp.maximum(dist, 1).astype(F32)
    large = max_exact + (jnp.log(d / max_exact) / math.log(MAX_DISTANCE / max_exact)
                         * (N_BUCKETS - max_exact)).astype(I32)
    large = jnp.clip(large, 0, N_BUCKETS - 1)
    return jnp.where(dist < max_exact, dist, large)


def _bias_of_dist(rel_bias, dist):
    bucket = _t5_bucket(dist)
    table = rel_bias.astype(F32) * LOG2E
    expand = (1,) * dist.ndim
    bias = jnp.zeros((A_HEADS,) + dist.shape, F32)
    for b in range(N_BUCKETS):
        bias = jnp.where((bucket == b)[None], table[b].reshape((A_HEADS,) + expand), bias)
    return jnp.where((dist >= 0)[None], bias, NEG_BIG)


def _far_bias(rel_bias, min_dist):
    assert int(16 + math.log(min_dist / 16) / math.log(MAX_DISTANCE / 16) * 16) >= N_BUCKETS - 1
    return rel_bias.astype(F32)[N_BUCKETS - 1] * LOG2E


def _permute_w_in(w_in):
    sizes = (A_WIDTH, A_WIDTH, A_WIDTH, IDX_HEADS * IDX_DIM, IDX_DIM, IDX_HEADS,
             B_WIDTH, B_WIDTH, B_WIDTH, C_HEADS * C_DK, C_WIDTH, C_HEADS * C_DK, C_WIDTH)
    offs = [0]
    for s in sizes:
        offs.append(offs[-1] + s)
    cols = lambda i: w_in[..., offs[i]:offs[i + 1]]
    padw = LANES - IDX_DIM - IDX_HEADS
    pad = jnp.zeros(w_in.shape[:-1] + (padw,), w_in.dtype)
    parts = [cols(i) for i in (0, 1, 2, 3, 6, 7, 8, 9, 10, 11, 12)] + [cols(4), cols(5), pad]
    return jnp.concatenate(parts, axis=-1).astype(BF16)


def kernel(x_prompt, x_sample, cache_k, cache_v, cache_kidx, state_conv, state_hgrn, page_table,
           w_in, w_out, conv_w, hgrn_lb_logits, hgrn_norm_g, rel_bias, norm1_g, norm2_g,
           w_ff1, w_ff2, final_g):
    batch, seq, d_model = x_prompt.shape
    dec_batch, dec_seq, _ = x_sample.shape
    depth = w_in.shape[0]
    n_pool, page = cache_k.shape[1], cache_k.shape[2]
    n_pages = page_table.shape[1]
    past = n_pages * page
    n_p, n_s = batch * seq, dec_batch * dec_seq
    n = n_p + n_s
    assert dec_seq == SUBLANES and seq % (2 * TQ) == 0 and seq % SCORE_CHUNK == 0
    tm = 512 if n % 512 == 0 else 256
    assert n % tm == 0 and n_p % tm == 0

    x = jnp.concatenate([x_prompt.reshape(n_p, d_model), x_sample.reshape(n_s, d_model)], axis=0)
    w_in_p = _permute_w_in(w_in)
    w_out_b = w_out.astype(BF16)
    w_ff1_b = w_ff1.astype(BF16)
    w_ff2_b = w_ff2.astype(BF16)
    sm = jax.nn.softmax(hgrn_lb_logits.astype(F32), axis=0)
    lb_all = jnp.cumsum(sm, axis=0) - sm[0]

    ck = jnp.transpose(cache_k, (0, 1, 3, 4, 2))
    cv = jnp.transpose(cache_v, (0, 1, 3, 4, 2))
    cki = jnp.transpose(cache_kidx, (0, 1, 3, 2))

    qi_ = jnp.arange(TQ)[None, :]
    kj_ = jnp.arange(2 * TQ)[:, None]
    tab_p = jnp.stack([_bias_of_dist(rel_bias, qi_ - kj_), _bias_of_dist(rel_bias, TQ + qi_ - kj_)])
    cfar_p = _far_bias(rel_bias, TQ + 1)
    rows = A_HEADS * dec_seq
    si = jnp.arange(dec_seq)[:, None]
    sj = jnp.arange(LANES)[None, :]
    tab_last = _bias_of_dist(rel_bias, page + si - sj).reshape(rows, LANES)
    d_new = jnp.where(sj < dec_seq, si - sj, -1)
    tab_new = _bias_of_dist(rel_bias, d_new).reshape(rows, LANES)
    cfar_s = jnp.broadcast_to(jnp.repeat(_far_bias(rel_bias, page + 1), dec_seq)[:, None], (rows, LANES))
    bd_s = (jnp.arange(rows)[:, None] // dec_seq == jnp.arange(A_WIDTH)[None, :] // HEAD_DIM).astype(F32)
    tabs_s = (tab_last, tab_new, cfar_s, bd_s)

    chunk_p = math.gcd(seq, HGRN_CHUNK)
    chunk_s = math.gcd(dec_seq, HGRN_CHUNK)
    tt_p = 512 if seq % 512 == 0 else seq
    nsb_s = math.gcd(dec_batch, MIX_SEQS_PER_STEP)
    consts_p = _mix_consts(chunk_p)
    consts_s = _mix_consts(chunk_s)
    conv0 = jnp.zeros((batch, CONV_K - 1, B_WIDTH), F32)
    hgrn0 = jnp.zeros((batch, C_HEADS, C_DK, C_DV), F32)

    kbuf = jnp.zeros((depth, batch, A_WIDTH, seq), F32)
    vbuf = jnp.zeros((depth, batch, A_WIDTH, seq), F32)
    ibuf = jnp.zeros((depth, batch, IDX_DIM, seq), F32)

    ks, vs, kis, cps, sps, css, sss = [], [], [], [], [], [], []
    for l in range(depth):
        p, qt, qit, kb, vt, ikwb, iwt, kbuf, vbuf, ibuf = _inproj(
            x, norm1_g[l], w_in_p[l], tm, l, kbuf, vbuf, ibuf)
        a_p = _dsa_prompt(iwt, qt, qit, kb, ikwb, vt, tab_p, cfar_p, batch, seq)
        a_s = _dsa_sample(p, n_p, cki, ck, cv, l, page_table, tabs_s, dec_batch, dec_seq)
        bc_p, conv_p, s_p = _mix(p, 0, batch, seq, tt_p, chunk_p, 1, conv0, hgrn0,
                                 conv_w[l], lb_all[l], hgrn_norm_g[l], consts_p)
        bc_s, conv_s, s_s = _mix(p, n_p, dec_batch, dec_seq, dec_seq, chunk_s, nsb_s, state_conv[l],
                                 state_hgrn[l], conv_w[l], lb_all[l], hgrn_norm_g[l], consts_s)
        x = _outffn(x, a_p, bc_p, a_s, bc_s, w_out_b[l], norm2_g[l], w_ff1_b[l], w_ff2_b[l], final_g,
                    tm, 1024, l == depth - 1)
        ks.append(p[n_p:, COL_K:COL_K + A_WIDTH])
        vs.append(p[n_p:, COL_V:COL_V + A_WIDTH])
        kis.append(p[n_p:, COL_IKW:COL_IKW + IDX_DIM])
        cps.append(conv_p); sps.append(s_p); css.append(conv_s); sss.append(s_s)

    heads = lambda buf: jnp.transpose(buf.reshape(depth, batch, A_HEADS, HEAD_DIM, seq), (0, 1, 4, 2, 3))
    return (
        x[:n_p].reshape(batch, seq, d_model),
        x[n_p:].reshape(dec_batch, dec_seq, d_model),
        heads(kbuf), heads(vbuf), jnp.transpose(ibuf, (0, 1, 3, 2)),
        jnp.stack(cps), jnp.stack(sps),
        jnp.stack(ks).reshape(depth, dec_batch, dec_seq, A_HEADS, HEAD_DIM),
        jnp.stack(vs).reshape(depth, dec_batch, dec_seq, A_HEADS, HEAD_DIM),
        jnp.stack(kis).reshape(depth, dec_batch, dec_seq, IDX_DIM),
        jnp.stack(css), jnp.stack(sss),
    )
```

```python
import functools
import math

import jax
import jax.numpy as jnp
from jax import lax
from jax.experimental import pallas as pl
from jax.experimental.pallas import tpu as pltpu

F32 = jnp.float32
BF16 = jnp.bfloat16
I32 = jnp.int32

HEAD_DIM = 64
A_HEADS = 8
A_WIDTH = A_HEADS * HEAD_DIM
IDX_HEADS = 8
IDX_DIM = 64
B_WIDTH = 256
C_HEADS = 4
C_DK = 64
C_DV = 64
C_WIDTH = C_HEADS * C_DV
TOPK_MAX = 256
CONV_K = 3
N_BUCKETS = 32
MAX_DISTANCE = 128
HGRN_CHUNK = 16
HGRN_UNROLL = 4
BISECT_GROUP = 4
MIX_SEQS_PER_STEP = 8
SAMPLE_SEQS_PER_STEP = 2
EPS = 1e-6
NEG_BIG = -1e30
LOG2E = 1.4426950408889634
INT_MIN = -(2 ** 31)
INT_MAX = 2 ** 31 - 1
HIGH_HALF = -(2 ** 16)
EXP_CLAMP = 80.0

LANES = 128
SUBLANES = 8
VMEM_LIMIT = 56 * 1024 * 1024

COL_Q, COL_K, COL_V, COL_QI = 0, 512, 1024, 1536
COL_BG, COL_CG, COL_BH, COL_HF, COL_HI, COL_HQ, COL_HG = (2048 + 256 * i for i in range(7))
COL_IKW = 3840
P_WIDTH = 3968

TQ = 128
SCORE_CHUNK = 512
ATT_CHUNK = 512


def _cparams(sem):
    return pltpu.CompilerParams(dimension_semantics=sem, vmem_limit_bytes=VMEM_LIMIT)


def _nt_dot(a, b):
    return lax.dot_general(a, b, (((1,), (1,)), ((), ())), preferred_element_type=F32)


def _tn_dot(a, b):
    return lax.dot_general(a, b, (((0,), (0,)), ((), ())), preferred_element_type=F32)


def _float_key(x):
    bits = pltpu.bitcast(x, I32)
    return bits ^ ((bits >> 31) & INT_MAX)


def _inproj_kernel(x_ref, g_ref, w_ref, ko_in, vo_in, io_in,
                   p_ref, qt_ref, qit_ref, kb_ref, vt_ref, ikw_ref, iwt_ref, ko_ref, vo_ref, io_ref,
                   *, n_prompt_blocks):
    del ko_in, vo_in, io_in
    x = x_ref[...]
    ms = jnp.mean(x * x, axis=-1, keepdims=True)
    xn = (x * lax.rsqrt(ms + EPS) * g_ref[...]).astype(BF16)
    res = jnp.dot(xn, w_ref[...], preferred_element_type=F32)
    p_ref[...] = res
    q_scale = (HEAD_DIM ** -0.5) * LOG2E
    w_scale = (IDX_HEADS ** -0.5) * (IDX_DIM ** -0.5)
    qt_ref[...] = (res[:, COL_Q:COL_Q + A_WIDTH] * q_scale).T.astype(BF16)
    qit_ref[...] = res[:, COL_QI:COL_QI + A_WIDTH].T.astype(BF16)
    k = res[:, COL_K:COL_K + A_WIDTH]
    kb_ref[...] = k.astype(BF16)
    vt = res[:, COL_V:COL_V + A_WIDTH].T
    vt_ref[...] = vt.astype(BF16)
    ikw = res[:, COL_IKW:COL_IKW + LANES]
    ikw_ref[...] = ikw.astype(BF16)
    ikwt = ikw.T
    iwt_ref[...] = ikwt[IDX_DIM:IDX_DIM + IDX_HEADS] * w_scale

    @pl.when(pl.program_id(0) < n_prompt_blocks)
    def _():
        ko_ref[...] = k.T
        vo_ref[...] = vt
        io_ref[...] = ikwt[:IDX_DIM]


def _inproj(x, g, w, tm, layer, kbuf, vbuf, ibuf):
    n, d = x.shape
    batch, seq = kbuf.shape[1], kbuf.shape[3]
    tpb = seq // tm
    npb = batch * tpb

    def slab_spec(width):
        def index(i):
            ii = jnp.minimum(i, npb - 1)
            return (layer, ii // tpb, 0, ii % tpb)
        return pl.BlockSpec((None, None, width, tm), index)

    any_spec = pl.BlockSpec(memory_space=pl.ANY)
    return pl.pallas_call(
        functools.partial(_inproj_kernel, n_prompt_blocks=npb),
        grid=(n // tm,),
        in_specs=[
            pl.BlockSpec((tm, d), lambda i: (i, 0)),
            pl.BlockSpec((1, d), lambda i: (0, 0)),
            pl.BlockSpec((d, P_WIDTH), lambda i: (0, 0)),
            any_spec, any_spec, any_spec,
        ],
        input_output_aliases={3: 7, 4: 8, 5: 9},
        out_specs=[
            pl.BlockSpec((tm, P_WIDTH), lambda i: (i, 0)),
            pl.BlockSpec((A_WIDTH, tm), lambda i: (0, i)),
            pl.BlockSpec((A_WIDTH, tm), lambda i: (0, i)),
            pl.BlockSpec((tm, A_WIDTH), lambda i: (i, 0)),
            pl.BlockSpec((A_WIDTH, tm), lambda i: (0, i)),
            pl.BlockSpec((tm, LANES), lambda i: (i, 0)),
            pl.BlockSpec((IDX_HEADS, tm), lambda i: (0, i)),
            slab_spec(A_WIDTH), slab_spec(A_WIDTH), slab_spec(IDX_DIM),
        ],
        out_shape=[
            jax.ShapeDtypeStruct((n, P_WIDTH), F32),
            jax.ShapeDtypeStruct((A_WIDTH, n), BF16),
            jax.ShapeDtypeStruct((A_WIDTH, n), BF16),
            jax.ShapeDtypeStruct((n, A_WIDTH), BF16),
            jax.ShapeDtypeStruct((A_WIDTH, n), BF16),
            jax.ShapeDtypeStruct((n, LANES), BF16),
            jax.ShapeDtypeStruct((IDX_HEADS, n), F32),
            jax.ShapeDtypeStruct(kbuf.shape, F32),
            jax.ShapeDtypeStruct(vbuf.shape, F32),
            jax.ShapeDtypeStruct(ibuf.shape, F32),
        ],
        compiler_params=_cparams(("arbitrary",)),
        name="inproj",
    )(x, g.reshape(1, d), w, kbuf, vbuf, ibuf)


def _outffn_kernel(x_ref, ap_ref, bcp_ref, as_ref, bcs_ref, wo_ref, g2_ref, w1_ref, w2_ref, gf_ref,
                   o_ref, acc_ref, hn_ref, *, final, n_prompt_blocks):
    i = pl.program_id(0)
    j = pl.program_id(1)

    def mix_in(a_ref, bc_ref):
        x1 = x_ref[...]
        x1 = x1 + jnp.dot(a_ref[...].astype(BF16), wo_ref[:A_WIDTH, :], preferred_element_type=F32)
        x1 = x1 + jnp.dot(bc_ref[...].astype(BF16), wo_ref[A_WIDTH:, :], preferred_element_type=F32)
        acc_ref[...] = x1
        ms = jnp.mean(x1 * x1, axis=-1, keepdims=True)
        hn_ref[...] = (x1 * lax.rsqrt(ms + EPS) * g2_ref[...]).astype(BF16)

    @pl.when(jnp.logical_and(j == 0, i < n_prompt_blocks))
    def _():
        mix_in(ap_ref, bcp_ref)

    @pl.when(jnp.logical_and(j == 0, i >= n_prompt_blocks))
    def _():
        mix_in(as_ref, bcs_ref)

    h = jnp.dot(hn_ref[...], w1_ref[...], preferred_element_type=F32)
    h = jnp.maximum(h, 0.0)
    h = (h * h).astype(BF16)
    acc_ref[...] += jnp.dot(h, w2_ref[...], preferred_element_type=F32)

    @pl.when(j == pl.num_programs(1) - 1)
    def _():
        y = acc_ref[...]
        if final:
            ms = jnp.mean(y * y, axis=-1, keepdims=True)
            y = y * lax.rsqrt(ms + EPS) * gf_ref[...]
        o_ref[...] = y


def _outffn(x, a_p, bc_p, a_s, bc_s, wo, g2, w1, w2, gf, tm, tf, final):
    n, d = x.shape
    dff = w1.shape[1]
    npb = a_p.shape[0] // tm
    prompt_rows = lambda i, j: (jnp.minimum(i, npb - 1), 0)
    sample_rows = lambda i, j: (jnp.maximum(i - npb, 0), 0)
    return pl.pallas_call(
        functools.partial(_outffn_kernel, final=final, n_prompt_blocks=npb),
        grid=(n // tm, dff // tf),
        in_specs=[
            pl.BlockSpec((tm, d), lambda i, j: (i, 0)),
            pl.BlockSpec((tm, A_WIDTH), prompt_rows),
            pl.BlockSpec((tm, B_WIDTH + C_WIDTH), prompt_rows),
            pl.BlockSpec((tm, A_WIDTH), sample_rows),
            pl.BlockSpec((tm, B_WIDTH + C_WIDTH), sample_rows),
            pl.BlockSpec(wo.shape, lambda i, j: (0, 0)),
            pl.BlockSpec((1, d), lambda i, j: (0, 0)),
            pl.BlockSpec((d, tf), lambda i, j: (0, j)),
            pl.BlockSpec((tf, d), lambda i, j: (j, 0)),
            pl.BlockSpec((1, d), lambda i, j: (0, 0)),
        ],
        out_specs=pl.BlockSpec((tm, d), lambda i, j: (i, 0)),
        out_shape=jax.ShapeDtypeStruct((n, d), F32),
        scratch_shapes=[pltpu.VMEM((tm, d), F32), pltpu.VMEM((tm, d), BF16)],
        compiler_params=_cparams(("arbitrary", "arbitrary")),
        name="outffn_final" if final else "outffn",
    )(x, a_p, bc_p, a_s, bc_s, wo, g2.reshape(1, d), w1, w2, gf.reshape(1, d))


def _topk_threshold(count_ge, count_eq_before, like, topk, n_pos):
    kf = float(topk)
    ans = _bisect_keys(count_ge, like, topk)
    c_ans = count_ge(ans)
    pos_bits = max(1, (n_pos - 1).bit_length())

    def tie_cut(_):
        need = kf - count_ge(ans + 1)

        def jbody(i, xcut):
            cand = xcut + (jnp.int32(1) << (pos_bits - 1 - i))
            return jnp.where(count_eq_before(ans, cand) < need, cand, xcut)

        cut = lax.fori_loop(0, pos_bits, jbody, jnp.zeros_like(like))
        return jnp.where(c_ans > kf, cut, n_pos)

    jcut = lax.cond(jnp.max(c_ans) > kf, tie_cut, lambda _: jnp.full_like(like, n_pos), 0)
    return ans, jcut


def _bisect_keys(count_ge, like, topk, count_hi=None):
    kf = float(topk)
    count_a = count_ge if count_hi is None else count_hi
    c0 = count_a(jnp.zeros_like(like))
    c1 = count_ge(jnp.ones_like(like))
    ans0 = jnp.where(c0 >= kf, 0, INT_MIN).astype(I32)
    open0 = jnp.where(jnp.logical_or(c0 == kf, jnp.logical_and(c1 < kf, c0 >= kf)), 0.0, 1.0)

    def step(i, ans, opn, count):
        cand = ans + (jnp.int32(1) << (30 - i))
        c = count(cand)
        take = jnp.logical_and(c >= kf, opn > 0.0)
        return jnp.where(take, cand, ans), jnp.where(c == kf, 0.0, opn)

    ans, opn = lax.fori_loop(0, 15, lambda i, st: step(i, *st, count_a), (ans0, open0))

    def group(st):
        g, ans, opn = st
        for u in range(BISECT_GROUP):
            ans, opn = step(15 + g * BISECT_GROUP + u, ans, opn, count_ge)
        return g + 1, ans, opn

    def pending(st):
        g, _, opn = st
        return jnp.logical_and(g < 16 // BISECT_GROUP, jnp.max(opn) > 0.0)

    return lax.while_loop(pending, group, (jnp.int32(0), ans, opn))[1]


def _dsa_prompt_kernel(cfar_ref, iwt_ref, qt_ref, qit_ref, kb_ref, ikw_ref, vt_ref, tab_ref, tril_ref,
                       o_ref, sc_ref, sch_ref, qbd_ref, qibd_ref, acc0_ref, acc1_ref, acc2_ref, acc3_ref,
                       ans_ref, *, seq, topk):
    qb = pl.program_id(1)
    tq = TQ
    qs = qb * tq
    npair = A_HEADS // 2

    zq = jnp.zeros((HEAD_DIM, tq), BF16)
    for j in range(npair):
        top = qt_ref[(2 * j) * HEAD_DIM:(2 * j + 1) * HEAD_DIM, :]
        bot = qt_ref[(2 * j + 1) * HEAD_DIM:(2 * j + 2) * HEAD_DIM, :]
        qbd_ref[j] = jnp.concatenate([jnp.concatenate([top, zq], axis=1),
                                      jnp.concatenate([zq, bot], axis=1)], axis=0)
        itop = qit_ref[(2 * j) * IDX_DIM:(2 * j + 1) * IDX_DIM, :]
        ibot = qit_ref[(2 * j + 1) * IDX_DIM:(2 * j + 2) * IDX_DIM, :]
        qibd_ref[j] = jnp.concatenate([jnp.concatenate([itop, ibot], axis=1),
                                       jnp.zeros((LANES - IDX_DIM, 2 * tq), BF16)], axis=0)

    wrow = iwt_ref[...]
    tpos = qs + lax.broadcasted_iota(I32, (1, tq), 1)

    def fold(x, op):
        while x.shape[0] > SUBLANES:
            half = x.shape[0] // 2
            x = op(x[:half], x[half:])
        return x

    n_sc = (qs + tq + SCORE_CHUNK - 1) // SCORE_CHUNK

    def score_body(c, carry):
        c0 = pl.multiple_of(c * SCORE_CHUNK, SCORE_CHUNK)
        ikc = ikw_ref[pl.ds(c0, SCORE_CHUNK), :]
        score = jnp.zeros((SCORE_CHUNK, tq), F32)
        for j in range(npair):
            s = jnp.dot(ikc, qibd_ref[j], preferred_element_type=F32)
            score = score + jnp.maximum(s[:, :tq], 0.0) * wrow[2 * j:2 * j + 1, :]
            score = score + jnp.maximum(s[:, tq:], 0.0) * wrow[2 * j + 1:2 * j + 2, :]
        pos = c0 + lax.broadcasted_iota(I32, (SCORE_CHUNK, tq), 0)
        masked = jnp.where(pos <= tpos, score, NEG_BIG)
        masked = jnp.where(masked == 0.0, 0.0, masked)
        bits = pltpu.bitcast(masked, I32)
        sc_ref[pl.ds(c0, SCORE_CHUNK), :] = bits ^ ((bits >> 31) & INT_MAX)
        sch_ref[pl.ds(c0, SCORE_CHUNK), :] = pltpu.bitcast(bits & HIGH_HALF, F32).astype(BF16)
        return carry

    lax.fori_loop(0, n_sc, score_body, 0)

    def chunk_sum(per_chunk, rows):
        def pair(c2, acc):
            return acc + per_chunk(2 * c2) + per_chunk(2 * c2 + 1)
        acc = lax.fori_loop(0, n_sc // 2, pair, jnp.zeros((rows, tq), F32))
        acc = lax.cond(n_sc % 2 == 1, lambda a: a + per_chunk(n_sc - 1), lambda a: a, acc)
        return jnp.sum(acc, axis=0, keepdims=True)

    def count_ge(cand):
        def per_chunk(c):
            c0 = pl.multiple_of(c * SCORE_CHUNK, SCORE_CHUNK)
            keys = sc_ref[pl.ds(c0, SCORE_CHUNK), :]
            return fold(jnp.where(keys >= cand, 1.0, 0.0), jnp.add)
        return chunk_sum(per_chunk, SUBLANES)

    def count_hi(cand):
        c16 = cand >> 16
        pat = (c16 ^ ((c16 >> 15) & 0x7FFF)) & 0xFFFF
        tiny = jnp.logical_and((pat & 0x7F80) == 0, (pat & 0x007F) != 0)
        pat = jnp.where(tiny, jnp.where((pat & 0x8000) != 0, 0, 0x0080), pat)
        cand_b = pltpu.bitcast(pat << 16, F32).astype(BF16)
        one, zero = jnp.ones((), BF16), jnp.zeros((), BF16)
        rows = 2 * SUBLANES

        def per_chunk(c):
            c0 = pl.multiple_of(c * SCORE_CHUNK, SCORE_CHUNK)
            hit = jnp.where(sch_ref[pl.ds(c0, SCORE_CHUNK), :] >= cand_b, one, zero)
            while hit.shape[0] > rows:
                half = hit.shape[0] // 2
                hit = hit[:half] + hit[half:]
            return hit.astype(F32)
        return chunk_sum(per_chunk, rows)

    ans_ref[...] = jnp.full((1, tq), INT_MIN + 1, I32)

    @pl.when(qs + tq > topk)
    def _():
        kf = float(topk)
        ans = _bisect_keys(count_ge, jnp.zeros((1, tq), I32), topk, count_hi)
        ans_ref[...] = ans

        @pl.when(jnp.max(count_ge(ans)) > kf)
        def _():
            need = kf - count_ge(ans + 1)

            def demote(c, base):
                c0 = pl.multiple_of(c * SCORE_CHUNK, SCORE_CHUNK)
                keys = sc_ref[pl.ds(c0, SCORE_CHUNK), :]
                tied = keys == ans
                rank = base + jnp.dot(tril_ref[...], jnp.where(tied, 1.0, 0.0).astype(BF16),
                                      preferred_element_type=F32)
                sc_ref[pl.ds(c0, SCORE_CHUNK), :] = jnp.where(
                    tied, jnp.where(rank > need, ans - 1, keys), keys)
                return rank[SCORE_CHUNK - 1:SCORE_CHUNK, :]

            lax.fori_loop(0, n_sc, demote, jnp.zeros((1, tq), F32))

    acc_refs = (acc0_ref, acc1_ref, acc2_ref, acc3_ref)
    for r in acc_refs:
        r[...] = jnp.zeros(r.shape, F32)
    ans = ans_ref[...]

    def attend_chunk(c0, width, limit, tsel, ms, lss):
        keys = sc_ref[pl.ds(c0, width), :]
        pos = c0 + lax.broadcasted_iota(I32, (width, tq), 0)
        sel = keys >= jnp.where(pos < limit, ans, INT_MAX)
        lgs = [jnp.dot(kb_ref[pl.ds(c0, width), j * LANES:(j + 1) * LANES], qbd_ref[j],
                       preferred_element_type=F32) for j in range(npair)]
        ms_new, lss_new = [], []
        for j in range(npair):
            ps, alphas = [], []
            for half in range(2):
                h = 2 * j + half
                lh = lgs[j][:, half * tq:(half + 1) * tq]
                if tsel is None:
                    cb = cfar_ref[h]
                else:
                    lh = lh + tab_ref[tsel, h]
                    cb = 0.0
                lh = jnp.where(sel, lh, NEG_BIG)
                cmax = jnp.max(fold(lh, jnp.maximum), axis=0, keepdims=True)
                m_new = jnp.maximum(ms[h], cmax + cb)
                alpha = jnp.exp2(ms[h] - m_new)
                p = jnp.exp2(lh - (m_new - cb))
                ms_new.append(m_new)
                lss_new.append(alpha * lss[h] + fold(p, jnp.add))
                ps.append(p.astype(BF16))
                alphas.append(alpha)
            vt = vt_ref[j * LANES:(j + 1) * LANES, pl.ds(c0, width)]
            pv = jnp.dot(vt, jnp.concatenate(ps, axis=1), preferred_element_type=F32)
            acc_refs[j][...] = acc_refs[j][...] * jnp.concatenate(alphas, axis=1) + pv
        return tuple(ms_new), tuple(lss_new)

    near0 = jnp.maximum(qb - 1, 0) * tq

    def far_body(c, carry):
        return attend_chunk(pl.multiple_of(c * ATT_CHUNK, ATT_CHUNK), ATT_CHUNK, near0, None, *carry)

    init = (tuple(jnp.full((1, tq), NEG_BIG, F32) for _ in range(A_HEADS)),
            tuple(jnp.zeros((SUBLANES, tq), F32) for _ in range(A_HEADS)))
    carry = lax.fori_loop(0, (near0 + ATT_CHUNK - 1) // ATT_CHUNK, far_body, init)
    _, lss = attend_chunk(pl.multiple_of(near0, tq), 2 * tq, seq, jnp.minimum(qb, 1), *carry)

    for j in range(npair):
        outs = []
        for half in range(2):
            inv = 1.0 / jnp.sum(lss[2 * j + half], axis=0, keepdims=True)
            blk = acc_refs[j][half * HEAD_DIM:(half + 1) * HEAD_DIM, half * tq:(half + 1) * tq]
            outs.append((blk * inv).T)
        o_ref[:, j * LANES:(j + 1) * LANES] = jnp.concatenate(outs, axis=1)


def _dsa_prompt(iwt, qt, qit, kb, ikwb, vt, tab, cfar, batch, seq):
    tq = TQ
    tril = (jnp.arange(SCORE_CHUNK)[None, :] <= jnp.arange(SCORE_CHUNK)[:, None]).astype(BF16)
    nq = seq // tq
    topk = min(TOPK_MAX, seq // 4)
    kern = functools.partial(_dsa_prompt_kernel, seq=seq, topk=topk)
    return pl.pallas_call(
        kern,
        grid=(batch, nq),
        in_specs=[
            pl.BlockSpec(memory_space=pltpu.SMEM),
            pl.BlockSpec((IDX_HEADS, tq), lambda b, i: (0, b * nq + i)),
            pl.BlockSpec((A_WIDTH, tq), lambda b, i: (0, b * nq + i)),
            pl.BlockSpec((A_WIDTH, tq), lambda b, i: (0, b * nq + i)),
            pl.BlockSpec((seq, A_WIDTH), lambda b, i: (b, 0)),
            pl.BlockSpec((seq, LANES), lambda b, i: (b, 0)),
            pl.BlockSpec((A_WIDTH, seq), lambda b, i: (0, b)),
            pl.BlockSpec(tab.shape, lambda b, i: (0, 0, 0, 0)),
            pl.BlockSpec(tril.shape, lambda b, i: (0, 0)),
        ],
        out_specs=pl.BlockSpec((tq, A_WIDTH), lambda b, i: (b * nq + i, 0)),
        out_shape=jax.ShapeDtypeStruct((batch * seq, A_WIDTH), F32),
        scratch_shapes=[
            pltpu.VMEM((seq, tq), I32),
            pltpu.VMEM((seq, tq), BF16),
            pltpu.VMEM((A_HEADS // 2, 2 * HEAD_DIM, 2 * tq), BF16),
            pltpu.VMEM((IDX_HEADS // 2, LANES, 2 * tq), BF16),
            pltpu.VMEM((2 * HEAD_DIM, 2 * tq), F32),
            pltpu.VMEM((2 * HEAD_DIM, 2 * tq), F32),
            pltpu.VMEM((2 * HEAD_DIM, 2 * tq), F32),
            pltpu.VMEM((2 * HEAD_DIM, 2 * tq), F32),
            pltpu.VMEM((1, tq), I32),
        ],
        compiler_params=_cparams(("arbitrary", "arbitrary")),
        name="dsa_prompt",
    )(cfar, iwt, qt, qit, kb, ikwb, vt, tab, tril)


def _dsa_sample_kernel(pt_ref, q_ref, kn_ref, vn_ref, qi_ref, ikw_ref, *rest, n_pages, page, topk, nb):
    npg = nb * n_pages
    kidx_refs, k_refs, v_refs = rest[:npg], rest[npg:2 * npg], rest[2 * npg:3 * npg]
    tab_last_ref, tab_new_ref, cfar_ref, bd_ref, o_ref = rest[3 * npg:]
    ds = q_ref.shape[0] // nb
    rows = A_HEADS * ds
    past = n_pages * page
    width = past + LANES
    w_scale = (IDX_HEADS ** -0.5) * (IDX_DIM ** -0.5)
    q_scale = (HEAD_DIM ** -0.5) * LOG2E

    def tile_heads(x):
        return jnp.broadcast_to(x[None], (A_HEADS,) + x.shape).reshape(rows, x.shape[1])

    def pad_rows(x):
        return jnp.concatenate([x, jnp.zeros((LANES - ds, x.shape[1]), x.dtype)], axis=0)

    qrow = lax.broadcasted_iota(I32, (ds, LANES), 0)
    kcol = lax.broadcasted_iota(I32, (ds, LANES), 1)

    def seq_scores(s):
        r = slice(s * ds, (s + 1) * ds)
        qi = qi_ref[r, :]
        q2 = jnp.concatenate([qi[:, h * IDX_DIM:(h + 1) * IDX_DIM] for h in range(IDX_HEADS)],
                             axis=0).astype(BF16)
        wi = ikw_ref[r, IDX_DIM:IDX_DIM + IDX_HEADS] * w_scale
        wcols = [jnp.broadcast_to(wi[:, h:h + 1], (ds, LANES)) for h in range(IDX_HEADS)]

        def head_sum(sc):
            out = jnp.zeros((ds, LANES), F32)
            for h in range(IDX_HEADS):
                out = out + jnp.maximum(sc[h * ds:(h + 1) * ds], 0.0) * wcols[h]
            return out

        scores = [head_sum(jnp.dot(q2, kidx_refs[s * n_pages + p][0].astype(BF16),
                                   preferred_element_type=F32)) for p in range(n_pages)]
        ik_new = pad_rows(ikw_ref[r, :IDX_DIM]).astype(BF16)
        s_new = jnp.where(kcol <= qrow, head_sum(_nt_dot(q2, ik_new)), NEG_BIG)
        return jnp.concatenate(scores + [s_new], axis=1)

    keys = _float_key(jnp.concatenate([seq_scores(s) for s in range(nb)], axis=0))
    pos = lax.broadcasted_iota(I32, (1, width), 1)

    def count_ge(cand):
        return jnp.sum(jnp.where(keys >= cand, 1.0, 0.0), axis=1, keepdims=True)

    def count_eq_before(a, xcut):
        hit = jnp.where(keys == a, jnp.where(pos < xcut, 1.0, 0.0), 0.0)
        return jnp.sum(hit, axis=1, keepdims=True)

    ans, jcut = _topk_threshold(count_ge, count_eq_before, jnp.zeros((nb * ds, 1), I32), topk, width)
    thr = jnp.where(pos <= jcut, ans - 1, ans)
    addmask_all = jnp.where(keys > thr, 0.0, NEG_BIG)

    bd = bd_ref[...]
    for s in range(nb):
        r = slice(s * ds, (s + 1) * ds)
        addmask = addmask_all[r]
        qbd = (tile_heads(q_ref[r, :] * q_scale) * bd).astype(BF16)
        logits = []
        for p in range(n_pages):
            kt = k_refs[s * n_pages + p][0].reshape(A_WIDTH, page).astype(BF16)
            lg = jnp.dot(qbd, kt, preferred_element_type=F32)
            lg = lg + (tab_last_ref[...] if p == n_pages - 1 else cfar_ref[...])
            logits.append(lg + tile_heads(addmask[:, p * page:(p + 1) * page]))
        lg = _nt_dot(qbd, pad_rows(kn_ref[r, :]).astype(BF16)) + tab_new_ref[...]
        logits.append(lg + tile_heads(addmask[:, past:]))
        m = functools.reduce(jnp.maximum, [jnp.max(l, axis=1, keepdims=True) for l in logits])
        acc = jnp.zeros((rows, A_WIDTH), F32)
        lsum = jnp.zeros((rows, 1), F32)
        for p in range(n_pages + 1):
            pr = jnp.exp2(logits[p] - m)
            lsum = lsum + jnp.sum(pr, axis=1, keepdims=True)
            if p < n_pages:
                vt = v_refs[s * n_pages + p][0].reshape(A_WIDTH, page).astype(BF16)
                acc = acc + _nt_dot(pr.astype(BF16), vt)
            else:
                acc = acc + jnp.dot(pr.astype(BF16), pad_rows(vn_ref[r, :]).astype(BF16),
                                    preferred_element_type=F32)
        acc = acc * (1.0 / lsum) * bd
        out = acc[0:ds]
        for h in range(1, A_HEADS):
            out = out + acc[h * ds:(h + 1) * ds]
        o_ref[r, :] = out


def _dsa_sample(p, row0, cache_kidx, cache_k, cache_v, layer, page_table, tabs, dec_batch, dec_seq):
    n_pages = page_table.shape[1]
    page = cache_k.shape[-1]
    past = n_pages * page
    topk = min(TOPK_MAX, (past + dec_seq) // 4)
    tab_last, tab_new, cfar, bd = tabs
    nb = math.gcd(dec_batch, SAMPLE_SEQS_PER_STEP)
    rows = nb * dec_seq
    rb = row0 // rows
    kern = functools.partial(_dsa_sample_kernel, n_pages=n_pages, page=page, topk=topk, nb=nb)

    def pspec(col, w):
        return pl.BlockSpec((rows, w), lambda b, pt: (rb + b, col // w))

    def page_spec(arr, s, pg):
        blk = (None, 1) + arr.shape[2:]
        zeros = (0,) * (arr.ndim - 2)
        return pl.BlockSpec(blk, lambda b, pt: (layer, pt[b * nb + s, pg]) + zeros)

    def const_spec(a):
        return pl.BlockSpec(a.shape, lambda b, pt: (0,) * a.ndim)

    in_specs = [pspec(COL_Q, A_WIDTH), pspec(COL_K, A_WIDTH), pspec(COL_V, A_WIDTH),
                pspec(COL_QI, A_WIDTH), pspec(COL_IKW, LANES)]
    for arr in (cache_kidx, cache_k, cache_v):
        in_specs += [page_spec(arr, s, g) for s in range(nb) for g in range(n_pages)]
    in_specs += [const_spec(a) for a in (tab_last, tab_new, cfar, bd)]
    grid_spec = pltpu.PrefetchScalarGridSpec(
        num_scalar_prefetch=1,
        grid=(dec_batch // nb,),
        in_specs=in_specs,
        out_specs=pl.BlockSpec((rows, A_WIDTH), lambda b, pt: (b, 0)),
    )
    npg = nb * n_pages
    args = [p] * 5 + [cache_kidx] * npg + [cache_k] * npg + [cache_v] * npg
    args += [tab_last, tab_new, cfar, bd]
    return pl.pallas_call(
        kern,
        grid_spec=grid_spec,
        out_shape=jax.ShapeDtypeStruct((dec_batch * dec_seq, A_WIDTH), F32),
        compiler_params=_cparams(("arbitrary",)),
        name="dsa_sample",
    )(page_table, *args)


def _mix_kernel(bg_ref, cg_ref, bh_ref, hf_ref, hi_ref, hq_ref, hg_ref, cs_ref, s0_ref,
                cw_ref, lb_ref, ng_ref, bdk_ref, bds_ref, cm_ref, seg_ref,
                o_ref, cso_ref, so_ref,
                uext_ref, st_ref, qi_ref, qa_ref, ka_ref, ks_ref, b_ref, oo_ref, *, chunk, nsb):
    t = pl.program_id(1)
    tt = bg_ref.shape[0] // nsb
    nblk = tt // chunk
    pad = SUBLANES
    lb = lb_ref[...]
    bdk = bdk_ref[...]
    bds = bds_ref[...]
    cmask = cm_ref[...]
    seg = seg_ref[...]
    unroll = math.gcd(nblk, HGRN_UNROLL)

    def tile_rows(x):
        return jnp.broadcast_to(x[None], (C_HEADS,) + x.shape).reshape(C_HEADS * chunk, x.shape[1])

    for s in range(nsb):
        tile = slice(s * tt, (s + 1) * tt)
        uext = uext_ref.at[s]
        st_s = st_ref.at[s]

        @pl.when(t == 0)
        def _():
            uext[pad - (CONV_K - 1):pad, :] = cs_ref[s]
            st_s[...] = jnp.zeros(st_s.shape, F32)
            for h in range(C_HEADS):
                st_s[h * C_DV:(h + 1) * C_DV, h * C_DK:(h + 1) * C_DK] = s0_ref[s, h].T

        u = cg_ref[tile, :] * bh_ref[tile, :]
        uext[pad:pad + tt, :] = u
        y = cw_ref[CONV_K - 1:CONV_K, :] * u
        for j in range(CONV_K - 1):
            y = y + cw_ref[j:j + 1, :] * uext[pad - (CONV_K - 1) + j:pad - (CONV_K - 1) + j + tt, :]
        o_ref[tile, :B_WIDTH] = bg_ref[tile, :] * y
        tail = uext[pad + tt - (CONV_K - 1):pad + tt, :]
        uext[pad - (CONV_K - 1):pad, :] = tail
        cso_ref[s] = tail

        fz = hf_ref[tile, :]
        f = lb + (1.0 - lb) * jax.nn.sigmoid(fz)
        logf = jnp.log(jnp.maximum(f, 1e-30))
        kk = (1.0 - lb) * jax.nn.sigmoid(-fz)
        qz = hq_ref[tile, :]
        qq = qz * jax.nn.sigmoid(qz)
        rowc = lax.broadcasted_iota(I32, (tt, 1), 0) % chunk
        b = logf
        d = 1
        while d < chunk:
            b = b + jnp.where(rowc >= d, pltpu.roll(b, d, 0), 0.0)
            d *= 2
        b3 = b.reshape(nblk, chunk, C_HEADS * C_DK)
        blast3 = b3[:, chunk - 1:chunk, :]
        bmid3 = b3[:, chunk // 2:chunk // 2 + 1, :]
        blast = jnp.broadcast_to(blast3, b3.shape).reshape(tt, C_HEADS * C_DK)
        bmid = jnp.broadcast_to(bmid3, b3.shape).reshape(tt, C_HEADS * C_DK)
        b_ref[tile, :] = b
        qi_ref[tile, :] = qq * jnp.exp(b)
        qa_ref[tile, :] = qq * jnp.exp(jnp.clip(b - bmid, -EXP_CLAMP, EXP_CLAMP))
        ka_ref[tile, :] = kk * jnp.exp(jnp.clip(bmid - b, -EXP_CLAMP, EXP_CLAMP))
        ks_ref[tile, :] = kk * jnp.exp(blast - b)

        def block_body(jo, st, s=s):
            for un in range(unroll):
                r0 = pl.multiple_of(s * tt + (jo * unroll + un) * chunk, chunk)
                rows = pl.ds(r0, chunk)
                v = hi_ref[rows, :]
                o_inter = _nt_dot(qi_ref[rows, :].astype(BF16), st.astype(BF16))
                kabd = (tile_rows(ka_ref[rows, :]) * bdk).astype(BF16)
                amat = _nt_dot(qa_ref[rows, :].astype(BF16), kabd) * cmask
                vbd = (tile_rows(v) * bdk).astype(BF16)
                o_intra = jnp.dot(amat.astype(BF16), vbd, preferred_element_type=F32)
                oo_ref[rows, :] = o_inter + o_intra
                dst = _tn_dot(v.astype(BF16), ks_ref[rows, :].astype(BF16))
                el = jnp.exp(b_ref[pl.ds(r0 + chunk - 1, 1), :])
                st = st * el + dst * bds
            return st

        st_s[...] = lax.fori_loop(0, nblk // unroll, block_body, st_s[...])

        o = oo_ref[tile, :]
        sq = o * o
        hi = sq.astype(BF16)
        lo = (sq - hi.astype(F32)).astype(BF16)
        ms = jnp.dot(hi, seg, preferred_element_type=F32) + jnp.dot(lo, seg, preferred_element_type=F32)
        gz = hg_ref[tile, :]
        o_ref[tile, B_WIDTH:] = o * lax.rsqrt(ms + EPS) * ng_ref[...] * (gz * jax.nn.sigmoid(gz))

        @pl.when(t == pl.num_programs(1) - 1)
        def _():
            stt = st_s[...].T
            for h in range(C_HEADS):
                so_ref[s, h] = stt[h * C_DK:(h + 1) * C_DK, h * C_DV:(h + 1) * C_DV]


def _mix(p, row0, nseq, seq, tt, chunk, nsb, conv_state, s0, cw, lb, ng, consts):
    bdk, bds, cmask, seg = consts
    nt = seq // tt
    assert nseq % nsb == 0 and (nsb == 1 or nt == 1)
    rows = nsb * tt
    rb = row0 // rows

    def pspec(col):
        return pl.BlockSpec((rows, B_WIDTH), lambda s, t: (rb + s * nt + t, col // B_WIDTH))

    def const_spec(a):
        return pl.BlockSpec(a.shape, lambda s, t: (0,) * a.ndim)

    w = C_HEADS * C_DK
    lb = lb.reshape(1, w)
    ng = ng.reshape(1, C_WIDTH)
    return pl.pallas_call(
        functools.partial(_mix_kernel, chunk=chunk, nsb=nsb),
        grid=(nseq // nsb, nt),
        in_specs=[pspec(c) for c in (COL_BG, COL_CG, COL_BH, COL_HF, COL_HI, COL_HQ, COL_HG)] + [
            pl.BlockSpec((nsb, CONV_K - 1, B_WIDTH), lambda s, t: (s, 0, 0)),
            pl.BlockSpec((nsb, C_HEADS, C_DK, C_DV), lambda s, t: (s, 0, 0, 0)),
            const_spec(cw), const_spec(lb), const_spec(ng),
            const_spec(bdk), const_spec(bds), const_spec(cmask), const_spec(seg),
        ],
        out_specs=[
            pl.BlockSpec((rows, B_WIDTH + C_WIDTH), lambda s, t: (s * nt + t, 0)),
            pl.BlockSpec((nsb, CONV_K - 1, B_WIDTH), lambda s, t: (s, 0, 0)),
            pl.BlockSpec((nsb, C_HEADS, C_DK, C_DV), lambda s, t: (s, 0, 0, 0)),
        ],
        out_shape=[
            jax.ShapeDtypeStruct((nseq * seq, B_WIDTH + C_WIDTH), F32),
            jax.ShapeDtypeStruct((nseq, CONV_K - 1, B_WIDTH), F32),
            jax.ShapeDtypeStruct((nseq, C_HEADS, C_DK, C_DV), F32),
        ],
        scratch_shapes=[
            pltpu.VMEM((nsb, tt + SUBLANES, B_WIDTH), F32),
            pltpu.VMEM((nsb, C_WIDTH, w), F32),
            pltpu.VMEM((rows, w), F32),
            pltpu.VMEM((rows, w), F32),
            pltpu.VMEM((rows, w), F32),
            pltpu.VMEM((rows, w), F32),
            pltpu.VMEM((rows, w), F32),
            pltpu.VMEM((rows, C_WIDTH), F32),
        ],
        compiler_params=_cparams(("arbitrary", "arbitrary")),
        name="mix_c%d" % chunk,
    )(p, p, p, p, p, p, p, conv_state, s0, cw, lb, ng, bdk, bds, cmask, seg)


def _mix_consts(chunk):
    w = C_HEADS * C_DK
    rh = jnp.arange(C_HEADS * chunk)[:, None] // chunk
    ch = jnp.arange(w)[None, :] // C_DK
    bdk = (rh == ch).astype(F32)
    hv = jnp.arange(C_WIDTH)[:, None] // C_DV
    bds = (hv == ch).astype(F32)
    tq = jnp.arange(chunk)[:, None]
    sk = jnp.arange(C_HEADS * chunk)[None, :] % chunk
    cmask = (sk <= tq).astype(F32)
    seg = (bds / C_DV).astype(BF16)
    return bdk, bds, cmask, seg


def _t5_bucket(dist):
    dist = jnp.maximum(dist, 0)
    max_exact = N_BUCKETS // 2
    d = jnp.maximum(dist, 1).astype(F32)
    large = max_exact + (jnp.log(d / max_exact) / math.log(MAX_DISTANCE / max_exact)
                         * (N_BUCKETS - max_exact)).astype(I32)
    large = jnp.clip(large, 0, N_BUCKETS - 1)
    return jnp.where(dist < max_exact, dist, large)


def _bias_of_dist(rel_bias, dist):
    bucket = _t5_bucket(dist)
    table = rel_bias.astype(F32) * LOG2E
    expand = (1,) * dist.ndim
    bias = jnp.zeros((A_HEADS,) + dist.shape, F32)
    for b in range(N_BUCKETS):
        bias = jnp.where((bucket == b)[None], table[b].reshape((A_HEADS,) + expand), bias)
    return jnp.where((dist >= 0)[None], bias, NEG_BIG)


def _far_bias(rel_bias, min_dist):
    assert int(16 + math.log(min_dist / 16) / math.log(MAX_DISTANCE / 16) * 16) >= N_BUCKETS - 1
    return rel_bias.astype(F32)[N_BUCKETS - 1] * LOG2E


def _permute_w_in(w_in):
    sizes = (A_WIDTH, A_WIDTH, A_WIDTH, IDX_HEADS * IDX_DIM, IDX_DIM, IDX_HEADS,
             B_WIDTH, B_WIDTH, B_WIDTH, C_HEADS * C_DK, C_WIDTH, C_HEADS * C_DK, C_WIDTH)
    offs = [0]
    for s in sizes:
        offs.append(offs[-1] + s)
    cols = lambda i: w_in[..., offs[i]:offs[i + 1]]
    padw = LANES - IDX_DIM - IDX_HEADS
    pad = jnp.zeros(w_in.shape[:-1] + (padw,), w_in.dtype)
    parts = [cols(i) for i in (0, 1, 2, 3, 6, 7, 8, 9, 10, 11, 12)] + [cols(4), cols(5), pad]
    return jnp.concatenate(parts, axis=-1).astype(BF16)


def kernel(x_prompt, x_sample, cache_k, cache_v, cache_kidx, state_conv, state_hgrn, page_table,
           w_in, w_out, conv_w, hgrn_lb_logits, hgrn_norm_g, rel_bias, norm1_g, norm2_g,
           w_ff1, w_ff2, final_g):
    batch, seq, d_model = x_prompt.shape
    dec_batch, dec_seq, _ = x_sample.shape
    depth = w_in.shape[0]
    n_pool, page = cache_k.shape[1], cache_k.shape[2]
    n_pages = page_table.shape[1]
    past = n_pages * page
    n_p, n_s = batch * seq, dec_batch * dec_seq
    n = n_p + n_s
    assert dec_seq == SUBLANES and seq % (2 * TQ) == 0 and seq % SCORE_CHUNK == 0
    tm = 512 if n % 512 == 0 else 256
    assert n % tm == 0 and n_p % tm == 0

    x = jnp.concatenate([x_prompt.reshape(n_p, d_model), x_sample.reshape(n_s, d_model)], axis=0)
    w_in_p = _permute_w_in(w_in)
    w_out_b = w_out.astype(BF16)
    w_ff1_b = w_ff1.astype(BF16)
    w_ff2_b = w_ff2.astype(BF16)
    sm = jax.nn.softmax(hgrn_lb_logits.astype(F32), axis=0)
    lb_all = jnp.cumsum(sm, axis=0) - sm[0]

    ck = jnp.transpose(cache_k, (0, 1, 3, 4, 2))
    cv = jnp.transpose(cache_v, (0, 1, 3, 4, 2))
    cki = jnp.transpose(cache_kidx, (0, 1, 3, 2))

    qi_ = jnp.arange(TQ)[None, :]
    kj_ = jnp.arange(2 * TQ)[:, None]
    tab_p = jnp.stack([_bias_of_dist(rel_bias, qi_ - kj_), _bias_of_dist(rel_bias, TQ + qi_ - kj_)])
    cfar_p = _far_bias(rel_bias, TQ + 1)
    rows = A_HEADS * dec_seq
    si = jnp.arange(dec_seq)[:, None]
    sj = jnp.arange(LANES)[None, :]
    tab_last = _bias_of_dist(rel_bias, page + si - sj).reshape(rows, LANES)
    d_new = jnp.where(sj < dec_seq, si - sj, -1)
    tab_new = _bias_of_dist(rel_bias, d_new).reshape(rows, LANES)
    cfar_s = jnp.broadcast_to(jnp.repeat(_far_bias(rel_bias, page + 1), dec_seq)[:, None], (rows, LANES))
    bd_s = (jnp.arange(rows)[:, None] // dec_seq == jnp.arange(A_WIDTH)[None, :] // HEAD_DIM).astype(F32)
    tabs_s = (tab_last, tab_new, cfar_s, bd_s)

    chunk_p = math.gcd(seq, HGRN_CHUNK)
    chunk_s = math.gcd(dec_seq, HGRN_CHUNK)
    tt_p = 512 if seq % 512 == 0 else seq
    nsb_s = math.gcd(dec_batch, MIX_SEQS_PER_STEP)
    consts_p = _mix_consts(chunk_p)
    consts_s = _mix_consts(chunk_s)
    conv0 = jnp.zeros((batch, CONV_K - 1, B_WIDTH), F32)
    hgrn0 = jnp.zeros((batch, C_HEADS, C_DK, C_DV), F32)

    kbuf = jnp.zeros((depth, batch, A_WIDTH, seq), F32)
    vbuf = jnp.zeros((depth, batch, A_WIDTH, seq), F32)
    ibuf = jnp.zeros((depth, batch, IDX_DIM, seq), F32)

    ks, vs, kis, cps, sps, css, sss = [], [], [], [], [], [], []
    for l in range(depth):
        p, qt, qit, kb, vt, ikwb, iwt, kbuf, vbuf, ibuf = _inproj(
            x, norm1_g[l], w_in_p[l], tm, l, kbuf, vbuf, ibuf)
        a_p = _dsa_prompt(iwt, qt, qit, kb, ikwb, vt, tab_p, cfar_p, batch, seq)
        a_s = _dsa_sample(p, n_p, cki, ck, cv, l, page_table, tabs_s, dec_batch, dec_seq)
        bc_p, conv_p, s_p = _mix(p, 0, batch, seq, tt_p, chunk_p, 1, conv0, hgrn0,
                                 conv_w[l], lb_all[l], hgrn_norm_g[l], consts_p)
        bc_s, conv_s, s_s = _mix(p, n_p, dec_batch, dec_seq, dec_seq, chunk_s, nsb_s, state_conv[l],
                                 state_hgrn[l], conv_w[l], lb_all[l], hgrn_norm_g[l], consts_s)
        x = _outffn(x, a_p, bc_p, a_s, bc_s, w_out_b[l], norm2_g[l], w_ff1_b[l], w_ff2_b[l], final_g,
                    tm, 1024, l == depth - 1)
        ks.append(p[n_p:, COL_K:COL_K + A_WIDTH])
        vs.append(p[n_p:, COL_V:COL_V + A_WIDTH])
        kis.append(p[n_p:, COL_IKW:COL_IKW + IDX_DIM])
        cps.append(conv_p); sps.append(s_p); css.append(conv_s); sss.append(s_s)

    heads = lambda buf: jnp.transpose(buf.reshape(depth, batch, A_HEADS, HEAD_DIM, seq), (0, 1, 4, 2, 3))
    return (
        x[:n_p].reshape(batch, seq, d_model),
        x[n_p:].reshape(dec_batch, dec_seq, d_model),
        heads(kbuf), heads(vbuf), jnp.transpose(ibuf, (0, 1, 3, 2)),
        jnp.stack(cps), jnp.stack(sps),
        jnp.stack(ks).reshape(depth, dec_batch, dec_seq, A_HEADS, HEAD_DIM),
        jnp.stack(vs).reshape(depth, dec_batch, dec_seq, A_HEADS, HEAD_DIM),
        jnp.stack(kis).reshape(depth, dec_batch, dec_seq, IDX_DIM),
        jnp.stack(css), jnp.stack(sss),
    )
```

```python
import functools
import math

import jax
import jax.numpy as jnp
from jax import lax
from jax.experimental import pallas as pl
from jax.experimental.pallas import tpu as pltpu

F32 = jnp.float32
BF16 = jnp.bfloat16
I32 = jnp.int32

HEAD_DIM = 64
A_HEADS = 8
A_WIDTH = A_HEADS * HEAD_DIM
IDX_HEADS = 8
IDX_DIM = 64
B_WIDTH = 256
C_HEADS = 4
C_DK = 64
C_DV = 64
C_WIDTH = C_HEADS * C_DV
TOPK_MAX = 256
CONV_K = 3
N_BUCKETS = 32
MAX_DISTANCE = 128
HGRN_CHUNK = 16
HGRN_UNROLL = 4
BISECT_GROUP = 4
MIX_SEQS_PER_STEP = 8
SAMPLE_SEQS_PER_STEP = 2
EPS = 1e-6
NEG_BIG = -1e30
LOG2E = 1.4426950408889634
INT_MIN = -(2 ** 31)
INT_MAX = 2 ** 31 - 1
EXP_CLAMP = 80.0

LANES = 128
SUBLANES = 8
VMEM_LIMIT = 56 * 1024 * 1024

COL_Q, COL_K, COL_V, COL_QI = 0, 512, 1024, 1536
COL_BG, COL_CG, COL_BH, COL_HF, COL_HI, COL_HQ, COL_HG = (2048 + 256 * i for i in range(7))
COL_IKW = 3840
P_WIDTH = 3968

TQ = 128
SCORE_CHUNK = 512
ATT_CHUNK = 512


def _cparams(sem):
    return pltpu.CompilerParams(dimension_semantics=sem, vmem_limit_bytes=VMEM_LIMIT)


def _nt_dot(a, b):
    return lax.dot_general(a, b, (((1,), (1,)), ((), ())), preferred_element_type=F32)


def _tn_dot(a, b):
    return lax.dot_general(a, b, (((0,), (0,)), ((), ())), preferred_element_type=F32)


def _float_key(x):
    bits = pltpu.bitcast(x, I32)
    return bits ^ ((bits >> 31) & INT_MAX)


def _inproj_kernel(x_ref, g_ref, w_ref, ko_in, vo_in, io_in,
                   p_ref, qt_ref, qit_ref, kb_ref, vt_ref, ikw_ref, iwt_ref, ko_ref, vo_ref, io_ref,
                   *, n_prompt_blocks):
    del ko_in, vo_in, io_in
    x = x_ref[...]
    ms = jnp.mean(x * x, axis=-1, keepdims=True)
    xn = (x * lax.rsqrt(ms + EPS) * g_ref[...]).astype(BF16)
    res = jnp.dot(xn, w_ref[...], preferred_element_type=F32)
    p_ref[...] = res
    q_scale = (HEAD_DIM ** -0.5) * LOG2E
    w_scale = (IDX_HEADS ** -0.5) * (IDX_DIM ** -0.5)
    qt_ref[...] = (res[:, COL_Q:COL_Q + A_WIDTH] * q_scale).T.astype(BF16)
    qit_ref[...] = res[:, COL_QI:COL_QI + A_WIDTH].T.astype(BF16)
    k = res[:, COL_K:COL_K + A_WIDTH]
    kb_ref[...] = k.astype(BF16)
    vt = res[:, COL_V:COL_V + A_WIDTH].T
    vt_ref[...] = vt.astype(BF16)
    ikw = res[:, COL_IKW:COL_IKW + LANES]
    ikw_ref[...] = ikw.astype(BF16)
    ikwt = ikw.T
    iwt_ref[...] = ikwt[IDX_DIM:IDX_DIM + IDX_HEADS] * w_scale

    @pl.when(pl.program_id(0) < n_prompt_blocks)
    def _():
        ko_ref[...] = k.T
        vo_ref[...] = vt
        io_ref[...] = ikwt[:IDX_DIM]


def _inproj(x, g, w, tm, layer, kbuf, vbuf, ibuf):
    n, d = x.shape
    batch, seq = kbuf.shape[1], kbuf.shape[3]
    tpb = seq // tm
    npb = batch * tpb

    def slab_spec(width):
        def index(i):
            ii = jnp.minimum(i, npb - 1)
            return (layer, ii // tpb, 0, ii % tpb)
        return pl.BlockSpec((None, None, width, tm), index)

    any_spec = pl.BlockSpec(memory_space=pl.ANY)
    return pl.pallas_call(
        functools.partial(_inproj_kernel, n_prompt_blocks=npb),
        grid=(n // tm,),
        in_specs=[
            pl.BlockSpec((tm, d), lambda i: (i, 0)),
            pl.BlockSpec((1, d), lambda i: (0, 0)),
            pl.BlockSpec((d, P_WIDTH), lambda i: (0, 0)),
            any_spec, any_spec, any_spec,
        ],
        input_output_aliases={3: 7, 4: 8, 5: 9},
        out_specs=[
            pl.BlockSpec((tm, P_WIDTH), lambda i: (i, 0)),
            pl.BlockSpec((A_WIDTH, tm), lambda i: (0, i)),
            pl.BlockSpec((A_WIDTH, tm), lambda i: (0, i)),
            pl.BlockSpec((tm, A_WIDTH), lambda i: (i, 0)),
            pl.BlockSpec((A_WIDTH, tm), lambda i: (0, i)),
            pl.BlockSpec((tm, LANES), lambda i: (i, 0)),
            pl.BlockSpec((IDX_HEADS, tm), lambda i: (0, i)),
            slab_spec(A_WIDTH), slab_spec(A_WIDTH), slab_spec(IDX_DIM),
        ],
        out_shape=[
            jax.ShapeDtypeStruct((n, P_WIDTH), F32),
            jax.ShapeDtypeStruct((A_WIDTH, n), BF16),
            jax.ShapeDtypeStruct((A_WIDTH, n), BF16),
            jax.ShapeDtypeStruct((n, A_WIDTH), BF16),
            jax.ShapeDtypeStruct((A_WIDTH, n), BF16),
            jax.ShapeDtypeStruct((n, LANES), BF16),
            jax.ShapeDtypeStruct((IDX_HEADS, n), F32),
            jax.ShapeDtypeStruct(kbuf.shape, F32),
            jax.ShapeDtypeStruct(vbuf.shape, F32),
            jax.ShapeDtypeStruct(ibuf.shape, F32),
        ],
        compiler_params=_cparams(("arbitrary",)),
        name="inproj",
    )(x, g.reshape(1, d), w, kbuf, vbuf, ibuf)


def _outffn_kernel(x_ref, ap_ref, bcp_ref, as_ref, bcs_ref, wo_ref, g2_ref, w1_ref, w2_ref, gf_ref,
                   o_ref, acc_ref, hn_ref, *, final, n_prompt_blocks):
    i = pl.program_id(0)
    j = pl.program_id(1)

    def mix_in(a_ref, bc_ref):
        x1 = x_ref[...]
        x1 = x1 + jnp.dot(a_ref[...].astype(BF16), wo_ref[:A_WIDTH, :], preferred_element_type=F32)
        x1 = x1 + jnp.dot(bc_ref[...].astype(BF16), wo_ref[A_WIDTH:, :], preferred_element_type=F32)
        acc_ref[...] = x1
        ms = jnp.mean(x1 * x1, axis=-1, keepdims=True)
        hn_ref[...] = (x1 * lax.rsqrt(ms + EPS) * g2_ref[...]).astype(BF16)

    @pl.when(jnp.logical_and(j == 0, i < n_prompt_blocks))
    def _():
        mix_in(ap_ref, bcp_ref)

    @pl.when(jnp.logical_and(j == 0, i >= n_prompt_blocks))
    def _():
        mix_in(as_ref, bcs_ref)

    h = jnp.dot(hn_ref[...], w1_ref[...], preferred_element_type=F32)
    h = jnp.maximum(h, 0.0)
    h = (h * h).astype(BF16)
    acc_ref[...] += jnp.dot(h, w2_ref[...], preferred_element_type=F32)

    @pl.when(j == pl.num_programs(1) - 1)
    def _():
        y = acc_ref[...]
        if final:
            ms = jnp.mean(y * y, axis=-1, keepdims=True)
            y = y * lax.rsqrt(ms + EPS) * gf_ref[...]
        o_ref[...] = y


def _outffn(x, a_p, bc_p, a_s, bc_s, wo, g2, w1, w2, gf, tm, tf, final):
    n, d = x.shape
    dff = w1.shape[1]
    npb = a_p.shape[0] // tm
    prompt_rows = lambda i, j: (jnp.minimum(i, npb - 1), 0)
    sample_rows = lambda i, j: (jnp.maximum(i - npb, 0), 0)
    return pl.pallas_call(
        functools.partial(_outffn_kernel, final=final, n_prompt_blocks=npb),
        grid=(n // tm, dff // tf),
        in_specs=[
            pl.BlockSpec((tm, d), lambda i, j: (i, 0)),
            pl.BlockSpec((tm, A_WIDTH), prompt_rows),
            pl.BlockSpec((tm, B_WIDTH + C_WIDTH), prompt_rows),
            pl.BlockSpec((tm, A_WIDTH), sample_rows),
            pl.BlockSpec((tm, B_WIDTH + C_WIDTH), sample_rows),
            pl.BlockSpec(wo.shape, lambda i, j: (0, 0)),
            pl.BlockSpec((1, d), lambda i, j: (0, 0)),
            pl.BlockSpec((d, tf), lambda i, j: (0, j)),
            pl.BlockSpec((tf, d), lambda i, j: (j, 0)),
            pl.BlockSpec((1, d), lambda i, j: (0, 0)),
        ],
        out_specs=pl.BlockSpec((tm, d), lambda i, j: (i, 0)),
        out_shape=jax.ShapeDtypeStruct((n, d), F32),
        scratch_shapes=[pltpu.VMEM((tm, d), F32), pltpu.VMEM((tm, d), BF16)],
        compiler_params=_cparams(("arbitrary", "arbitrary")),
        name="outffn_final" if final else "outffn",
    )(x, a_p, bc_p, a_s, bc_s, wo, g2.reshape(1, d), w1, w2, gf.reshape(1, d))


def _topk_threshold(count_ge, count_eq_before, like, topk, n_pos):
    kf = float(topk)
    ans = _bisect_keys(count_ge, like, topk)
    c_ans = count_ge(ans)
    pos_bits = max(1, (n_pos - 1).bit_length())

    def tie_cut(_):
        need = kf - count_ge(ans + 1)

        def jbody(i, xcut):
            cand = xcut + (jnp.int32(1) << (pos_bits - 1 - i))
            return jnp.where(count_eq_before(ans, cand) < need, cand, xcut)

        cut = lax.fori_loop(0, pos_bits, jbody, jnp.zeros_like(like))
        return jnp.where(c_ans > kf, cut, n_pos)

    jcut = lax.cond(jnp.max(c_ans) > kf, tie_cut, lambda _: jnp.full_like(like, n_pos), 0)
    return ans, jcut


def _bisect_keys(count_ge, like, topk):
    kf = float(topk)
    c0 = count_ge(jnp.zeros_like(like))
    c1 = count_ge(jnp.ones_like(like))
    ans0 = jnp.where(c0 >= kf, 0, INT_MIN).astype(I32)
    open0 = jnp.where(jnp.logical_or(c0 == kf, jnp.logical_and(c1 < kf, c0 >= kf)), 0.0, 1.0)

    def step(i, ans, opn):
        cand = ans + (jnp.int32(1) << (30 - i))
        c = count_ge(cand)
        take = jnp.logical_and(c >= kf, opn > 0.0)
        return jnp.where(take, cand, ans), jnp.where(c == kf, 0.0, opn)

    def group(st):
        g, ans, opn = st
        for u in range(BISECT_GROUP):
            ans, opn = step(g * BISECT_GROUP + u, ans, opn)
        return g + 1, ans, opn

    def pending(st):
        g, _, opn = st
        return jnp.logical_and(g < 31 // BISECT_GROUP, jnp.max(opn) > 0.0)

    g, ans, opn = lax.while_loop(pending, group, (jnp.int32(0), ans0, open0))

    def tail(_):
        return lax.fori_loop(g * BISECT_GROUP, 31, lambda i, st: step(i, *st), (ans, opn))[0]

    return lax.cond(jnp.max(opn) > 0.0, tail, lambda _: ans, 0)


def _dsa_prompt_kernel(cfar_ref, iwt_ref, qt_ref, qit_ref, kb_ref, ikw_ref, vt_ref, tab_ref, tril_ref,
                       o_ref, sc_ref, qbd_ref, qibd_ref, acc0_ref, acc1_ref, acc2_ref, acc3_ref,
                       ans_ref, *, seq, topk):
    qb = pl.program_id(1)
    tq = TQ
    qs = qb * tq
    npair = A_HEADS // 2

    zq = jnp.zeros((HEAD_DIM, tq), BF16)
    for j in range(npair):
        top = qt_ref[(2 * j) * HEAD_DIM:(2 * j + 1) * HEAD_DIM, :]
        bot = qt_ref[(2 * j + 1) * HEAD_DIM:(2 * j + 2) * HEAD_DIM, :]
        qbd_ref[j] = jnp.concatenate([jnp.concatenate([top, zq], axis=1),
                                      jnp.concatenate([zq, bot], axis=1)], axis=0)
        itop = qit_ref[(2 * j) * IDX_DIM:(2 * j + 1) * IDX_DIM, :]
        ibot = qit_ref[(2 * j + 1) * IDX_DIM:(2 * j + 2) * IDX_DIM, :]
        qibd_ref[j] = jnp.concatenate([jnp.concatenate([itop, ibot], axis=1),
                                       jnp.zeros((LANES - IDX_DIM, 2 * tq), BF16)], axis=0)

    wrow = iwt_ref[...]
    tpos = qs + lax.broadcasted_iota(I32, (1, tq), 1)

    def fold(x, op):
        while x.shape[0] > SUBLANES:
            half = x.shape[0] // 2
            x = op(x[:half], x[half:])
        return x

    n_sc = (qs + tq + SCORE_CHUNK - 1) // SCORE_CHUNK

    def score_body(c, carry):
        c0 = pl.multiple_of(c * SCORE_CHUNK, SCORE_CHUNK)
        ikc = ikw_ref[pl.ds(c0, SCORE_CHUNK), :]
        score = jnp.zeros((SCORE_CHUNK, tq), F32)
        for j in range(npair):
            s = jnp.dot(ikc, qibd_ref[j], preferred_element_type=F32)
            score = score + jnp.maximum(s[:, :tq], 0.0) * wrow[2 * j:2 * j + 1, :]
            score = score + jnp.maximum(s[:, tq:], 0.0) * wrow[2 * j + 1:2 * j + 2, :]
        pos = c0 + lax.broadcasted_iota(I32, (SCORE_CHUNK, tq), 0)
        sc_ref[pl.ds(c0, SCORE_CHUNK), :] = _float_key(jnp.where(pos <= tpos, score, NEG_BIG))
        return carry

    lax.fori_loop(0, n_sc, score_body, 0)

    def count_ge(cand):
        def body(c, acc):
            c0 = pl.multiple_of(c * SCORE_CHUNK, SCORE_CHUNK)
            keys = sc_ref[pl.ds(c0, SCORE_CHUNK), :]
            return acc + fold(jnp.where(keys >= cand, 1.0, 0.0), jnp.add)
        acc = lax.fori_loop(0, n_sc, body, jnp.zeros((SUBLANES, tq), F32))
        return jnp.sum(acc, axis=0, keepdims=True)

    ans_ref[...] = jnp.full((1, tq), INT_MIN + 1, I32)

    @pl.when(qs + tq > topk)
    def _():
        kf = float(topk)
        ans = _bisect_keys(count_ge, jnp.zeros((1, tq), I32), topk)
        ans_ref[...] = ans

        @pl.when(jnp.max(count_ge(ans)) > kf)
        def _():
            need = kf - count_ge(ans + 1)

            def demote(c, base):
                c0 = pl.multiple_of(c * SCORE_CHUNK, SCORE_CHUNK)
                keys = sc_ref[pl.ds(c0, SCORE_CHUNK), :]
                tied = keys == ans
                rank = base + jnp.dot(tril_ref[...], jnp.where(tied, 1.0, 0.0).astype(BF16),
                                      preferred_element_type=F32)
                sc_ref[pl.ds(c0, SCORE_CHUNK), :] = jnp.where(
                    tied, jnp.where(rank > need, ans - 1, keys), keys)
                return rank[SCORE_CHUNK - 1:SCORE_CHUNK, :]

            lax.fori_loop(0, n_sc, demote, jnp.zeros((1, tq), F32))

    acc_refs = (acc0_ref, acc1_ref, acc2_ref, acc3_ref)
    for r in acc_refs:
        r[...] = jnp.zeros(r.shape, F32)
    ans = ans_ref[...]

    def attend_chunk(c0, width, limit, tsel, ms, lss):
        keys = sc_ref[pl.ds(c0, width), :]
        pos = c0 + lax.broadcasted_iota(I32, (width, tq), 0)
        sel = keys >= jnp.where(pos < limit, ans, INT_MAX)
        def logits(j):
            return jnp.dot(kb_ref[pl.ds(c0, width), j * LANES:(j + 1) * LANES], qbd_ref[j],
                           preferred_element_type=F32)

        ms_new, lss_new = [], []
        lg_next = logits(0)
        for j in range(npair):
            lg = lg_next
            if j + 1 < npair:
                lg_next = logits(j + 1)
            ps, alphas = [], []
            for half in range(2):
                h = 2 * j + half
                lh = lg[:, half * tq:(half + 1) * tq]
                if tsel is None:
                    cb = cfar_ref[h]
                else:
                    lh = lh + tab_ref[tsel, h]
                    cb = 0.0
                lh = jnp.where(sel, lh, NEG_BIG)
                cmax = jnp.max(fold(lh, jnp.maximum), axis=0, keepdims=True)
                m_new = jnp.maximum(ms[h], cmax + cb)
                alpha = jnp.exp2(ms[h] - m_new)
                p = jnp.exp2(lh - (m_new - cb))
                ms_new.append(m_new)
                lss_new.append(alpha * lss[h] + fold(p, jnp.add))
                ps.append(p.astype(BF16))
                alphas.append(alpha)
            vt = vt_ref[j * LANES:(j + 1) * LANES, pl.ds(c0, width)]
            pv = jnp.dot(vt, jnp.concatenate(ps, axis=1), preferred_element_type=F32)
            acc_refs[j][...] = acc_refs[j][...] * jnp.concatenate(alphas, axis=1) + pv
        return tuple(ms_new), tuple(lss_new)

    near0 = jnp.maximum(qb - 1, 0) * tq

    def far_body(c, carry):
        return attend_chunk(pl.multiple_of(c * ATT_CHUNK, ATT_CHUNK), ATT_CHUNK, near0, None, *carry)

    init = (tuple(jnp.full((1, tq), NEG_BIG, F32) for _ in range(A_HEADS)),
            tuple(jnp.zeros((SUBLANES, tq), F32) for _ in range(A_HEADS)))
    carry = lax.fori_loop(0, (near0 + ATT_CHUNK - 1) // ATT_CHUNK, far_body, init)
    _, lss = attend_chunk(pl.multiple_of(near0, tq), 2 * tq, seq, jnp.minimum(qb, 1), *carry)

    for j in range(npair):
        outs = []
        for half in range(2):
            inv = 1.0 / jnp.sum(lss[2 * j + half], axis=0, keepdims=True)
            blk = acc_refs[j][half * HEAD_DIM:(half + 1) * HEAD_DIM, half * tq:(half + 1) * tq]
            outs.append((blk * inv).T)
        o_ref[:, j * LANES:(j + 1) * LANES] = jnp.concatenate(outs, axis=1)


def _dsa_prompt(iwt, qt, qit, kb, ikwb, vt, tab, cfar, batch, seq):
    tq = TQ
    tril = (jnp.arange(SCORE_CHUNK)[None, :] <= jnp.arange(SCORE_CHUNK)[:, None]).astype(BF16)
    nq = seq // tq
    topk = min(TOPK_MAX, seq // 4)
    kern = functools.partial(_dsa_prompt_kernel, seq=seq, topk=topk)
    return pl.pallas_call(
        kern,
        grid=(batch, nq),
        in_specs=[
            pl.BlockSpec(memory_space=pltpu.SMEM),
            pl.BlockSpec((IDX_HEADS, tq), lambda b, i: (0, b * nq + i)),
            pl.BlockSpec((A_WIDTH, tq), lambda b, i: (0, b * nq + i)),
            pl.BlockSpec((A_WIDTH, tq), lambda b, i: (0, b * nq + i)),
            pl.BlockSpec((seq, A_WIDTH), lambda b, i: (b, 0)),
            pl.BlockSpec((seq, LANES), lambda b, i: (b, 0)),
            pl.BlockSpec((A_WIDTH, seq), lambda b, i: (0, b)),
            pl.BlockSpec(tab.shape, lambda b, i: (0, 0, 0, 0)),
            pl.BlockSpec(tril.shape, lambda b, i: (0, 0)),
        ],
        out_specs=pl.BlockSpec((tq, A_WIDTH), lambda b, i: (b * nq + i, 0)),
        out_shape=jax.ShapeDtypeStruct((batch * seq, A_WIDTH), F32),
        scratch_shapes=[
            pltpu.VMEM((seq, tq), I32),
            pltpu.VMEM((A_HEADS // 2, 2 * HEAD_DIM, 2 * tq), BF16),
            pltpu.VMEM((IDX_HEADS // 2, LANES, 2 * tq), BF16),
            pltpu.VMEM((2 * HEAD_DIM, 2 * tq), F32),
            pltpu.VMEM((2 * HEAD_DIM, 2 * tq), F32),
            pltpu.VMEM((2 * HEAD_DIM, 2 * tq), F32),
            pltpu.VMEM((2 * HEAD_DIM, 2 * tq), F32),
            pltpu.VMEM((1, tq), I32),
        ],
        compiler_params=_cparams(("arbitrary", "arbitrary")),
        name="dsa_prompt",
    )(cfar, iwt, qt, qit, kb, ikwb, vt, tab, tril)


def _dsa_sample_kernel(pt_ref, q_ref, kn_ref, vn_ref, qi_ref, ikw_ref, *rest, n_pages, page, topk, nb):
    npg = nb * n_pages
    kidx_refs, k_refs, v_refs = rest[:npg], rest[npg:2 * npg], rest[2 * npg:3 * npg]
    tab_last_ref, tab_new_ref, cfar_ref, bd_ref, o_ref = rest[3 * npg:]
    ds = q_ref.shape[0] // nb
    rows = A_HEADS * ds
    past = n_pages * page
    width = past + LANES
    w_scale = (IDX_HEADS ** -0.5) * (IDX_DIM ** -0.5)
    q_scale = (HEAD_DIM ** -0.5) * LOG2E

    def tile_heads(x):
        return jnp.broadcast_to(x[None], (A_HEADS,) + x.shape).reshape(rows, x.shape[1])

    def pad_rows(x):
        return jnp.concatenate([x, jnp.zeros((LANES - ds, x.shape[1]), x.dtype)], axis=0)

    qrow = lax.broadcasted_iota(I32, (ds, LANES), 0)
    kcol = lax.broadcasted_iota(I32, (ds, LANES), 1)

    def seq_scores(s):
        r = slice(s * ds, (s + 1) * ds)
        qi = qi_ref[r, :]
        q2 = jnp.concatenate([qi[:, h * IDX_DIM:(h + 1) * IDX_DIM] for h in range(IDX_HEADS)],
                             axis=0).astype(BF16)
        wi = ikw_ref[r, IDX_DIM:IDX_DIM + IDX_HEADS] * w_scale
        wcols = [jnp.broadcast_to(wi[:, h:h + 1], (ds, LANES)) for h in range(IDX_HEADS)]

        def head_sum(sc):
            out = jnp.zeros((ds, LANES), F32)
            for h in range(IDX_HEADS):
                out = out + jnp.maximum(sc[h * ds:(h + 1) * ds], 0.0) * wcols[h]
            return out

        scores = [head_sum(jnp.dot(q2, kidx_refs[s * n_pages + p][0].astype(BF16),
                                   preferred_element_type=F32)) for p in range(n_pages)]
        ik_new = pad_rows(ikw_ref[r, :IDX_DIM]).astype(BF16)
        s_new = jnp.where(kcol <= qrow, head_sum(_nt_dot(q2, ik_new)), NEG_BIG)
        return jnp.concatenate(scores + [s_new], axis=1)

    keys = _float_key(jnp.concatenate([seq_scores(s) for s in range(nb)], axis=0))
    pos = lax.broadcasted_iota(I32, (1, width), 1)

    def count_ge(cand):
        return jnp.sum(jnp.where(keys >= cand, 1.0, 0.0), axis=1, keepdims=True)

    def count_eq_before(a, xcut):
        hit = jnp.where(keys == a, jnp.where(pos < xcut, 1.0, 0.0), 0.0)
        return jnp.sum(hit, axis=1, keepdims=True)

    ans, jcut = _topk_threshold(count_ge, count_eq_before, jnp.zeros((nb * ds, 1), I32), topk, width)
    thr = jnp.where(pos <= jcut, ans - 1, ans)
    addmask_all = jnp.where(keys > thr, 0.0, NEG_BIG)

    bd = bd_ref[...]
    for s in range(nb):
        r = slice(s * ds, (s + 1) * ds)
        addmask = addmask_all[r]
        qbd = (tile_heads(q_ref[r, :] * q_scale) * bd).astype(BF16)
        logits = []
        for p in range(n_pages):
            kt = k_refs[s * n_pages + p][0].reshape(A_WIDTH, page).astype(BF16)
            lg = jnp.dot(qbd, kt, preferred_element_type=F32)
            lg = lg + (tab_last_ref[...] if p == n_pages - 1 else cfar_ref[...])
            logits.append(lg + tile_heads(addmask[:, p * page:(p + 1) * page]))
        lg = _nt_dot(qbd, pad_rows(kn_ref[r, :]).astype(BF16)) + tab_new_ref[...]
        logits.append(lg + tile_heads(addmask[:, past:]))
        m = functools.reduce(jnp.maximum, [jnp.max(l, axis=1, keepdims=True) for l in logits])
        acc = jnp.zeros((rows, A_WIDTH), F32)
        lsum = jnp.zeros((rows, 1), F32)
        for p in range(n_pages + 1):
            pr = jnp.exp2(logits[p] - m)
            lsum = lsum + jnp.sum(pr, axis=1, keepdims=True)
            if p < n_pages:
                vt = v_refs[s * n_pages + p][0].reshape(A_WIDTH, page).astype(BF16)
                acc = acc + _nt_dot(pr.astype(BF16), vt)
            else:
                acc = acc + jnp.dot(pr.astype(BF16), pad_rows(vn_ref[r, :]).astype(BF16),
                                    preferred_element_type=F32)
        acc = acc * (1.0 / lsum) * bd
        out = acc[0:ds]
        for h in range(1, A_HEADS):
            out = out + acc[h * ds:(h + 1) * ds]
        o_ref[r, :] = out


def _dsa_sample(p, row0, cache_kidx, cache_k, cache_v, layer, page_table, tabs, dec_batch, dec_seq):
    n_pages = page_table.shape[1]
    page = cache_k.shape[-1]
    past = n_pages * page
    topk = min(TOPK_MAX, (past + dec_seq) // 4)
    tab_last, tab_new, cfar, bd = tabs
    nb = math.gcd(dec_batch, SAMPLE_SEQS_PER_STEP)
    rows = nb * dec_seq
    rb = row0 // rows
    kern = functools.partial(_dsa_sample_kernel, n_pages=n_pages, page=page, topk=topk, nb=nb)

    def pspec(col, w):
        return pl.BlockSpec((rows, w), lambda b, pt: (rb + b, col // w))

    def page_spec(arr, s, pg):
        blk = (None, 1) + arr.shape[2:]
        zeros = (0,) * (arr.ndim - 2)
        return pl.BlockSpec(blk, lambda b, pt: (layer, pt[b * nb + s, pg]) + zeros)

    def const_spec(a):
        return pl.BlockSpec(a.shape, lambda b, pt: (0,) * a.ndim)

    in_specs = [pspec(COL_Q, A_WIDTH), pspec(COL_K, A_WIDTH), pspec(COL_V, A_WIDTH),
                pspec(COL_QI, A_WIDTH), pspec(COL_IKW, LANES)]
    for arr in (cache_kidx, cache_k, cache_v):
        in_specs += [page_spec(arr, s, g) for s in range(nb) for g in range(n_pages)]
    in_specs += [const_spec(a) for a in (tab_last, tab_new, cfar, bd)]
    grid_spec = pltpu.PrefetchScalarGridSpec(
        num_scalar_prefetch=1,
        grid=(dec_batch // nb,),
        in_specs=in_specs,
        out_specs=pl.BlockSpec((rows, A_WIDTH), lambda b, pt: (b, 0)),
    )
    npg = nb * n_pages
    args = [p] * 5 + [cache_kidx] * npg + [cache_k] * npg + [cache_v] * npg
    args += [tab_last, tab_new, cfar, bd]
    return pl.pallas_call(
        kern,
        grid_spec=grid_spec,
        out_shape=jax.ShapeDtypeStruct((dec_batch * dec_seq, A_WIDTH), F32),
        compiler_params=_cparams(("arbitrary",)),
        name="dsa_sample",
    )(page_table, *args)


def _mix_kernel(bg_ref, cg_ref, bh_ref, hf_ref, hi_ref, hq_ref, hg_ref, cs_ref, s0_ref,
                cw_ref, lb_ref, ng_ref, bdk_ref, bds_ref, cm_ref, seg_ref,
                o_ref, cso_ref, so_ref,
                uext_ref, st_ref, qi_ref, qa_ref, ka_ref, ks_ref, b_ref, oo_ref, *, chunk, nsb):
    t = pl.program_id(1)
    tt = bg_ref.shape[0] // nsb
    nblk = tt // chunk
    pad = SUBLANES
    lb = lb_ref[...]
    bdk = bdk_ref[...]
    bds = bds_ref[...]
    cmask = cm_ref[...]
    seg = seg_ref[...]
    unroll = math.gcd(nblk, HGRN_UNROLL)

    def tile_rows(x):
        return jnp.broadcast_to(x[None], (C_HEADS,) + x.shape).reshape(C_HEADS * chunk, x.shape[1])

    for s in range(nsb):
        tile = slice(s * tt, (s + 1) * tt)
        uext = uext_ref.at[s]
        st_s = st_ref.at[s]

        @pl.when(t == 0)
        def _():
            uext[pad - (CONV_K - 1):pad, :] = cs_ref[s]
            st_s[...] = jnp.zeros(st_s.shape, F32)
            for h in range(C_HEADS):
                st_s[h * C_DV:(h + 1) * C_DV, h * C_DK:(h + 1) * C_DK] = s0_ref[s, h].T

        u = cg_ref[tile, :] * bh_ref[tile, :]
        uext[pad:pad + tt, :] = u
        y = cw_ref[CONV_K - 1:CONV_K, :] * u
        for j in range(CONV_K - 1):
            y = y + cw_ref[j:j + 1, :] * uext[pad - (CONV_K - 1) + j:pad - (CONV_K - 1) + j + tt, :]
        o_ref[tile, :B_WIDTH] = bg_ref[tile, :] * y
        tail = uext[pad + tt - (CONV_K - 1):pad + tt, :]
        uext[pad - (CONV_K - 1):pad, :] = tail
        cso_ref[s] = tail

        fz = hf_ref[tile, :]
        f = lb + (1.0 - lb) * jax.nn.sigmoid(fz)
        logf = jnp.log(jnp.maximum(f, 1e-30))
        kk = (1.0 - lb) * jax.nn.sigmoid(-fz)
        qz = hq_ref[tile, :]
        qq = qz * jax.nn.sigmoid(qz)
        rowc = lax.broadcasted_iota(I32, (tt, 1), 0) % chunk
        b = logf
        d = 1
        while d < chunk:
            b = b + jnp.where(rowc >= d, pltpu.roll(b, d, 0), 0.0)
            d *= 2
        b3 = b.reshape(nblk, chunk, C_HEADS * C_DK)
        blast3 = b3[:, chunk - 1:chunk, :]
        bmid3 = b3[:, chunk // 2:chunk // 2 + 1, :]
        blast = jnp.broadcast_to(blast3, b3.shape).reshape(tt, C_HEADS * C_DK)
        bmid = jnp.broadcast_to(bmid3, b3.shape).reshape(tt, C_HEADS * C_DK)
        b_ref[tile, :] = b
        qi_ref[tile, :] = qq * jnp.exp(b)
        qa_ref[tile, :] = qq * jnp.exp(jnp.clip(b - bmid, -EXP_CLAMP, EXP_CLAMP))
        ka_ref[tile, :] = kk * jnp.exp(jnp.clip(bmid - b, -EXP_CLAMP, EXP_CLAMP))
        ks_ref[tile, :] = kk * jnp.exp(blast - b)

        def block_body(jo, st, s=s):
            for un in range(unroll):
                r0 = pl.multiple_of(s * tt + (jo * unroll + un) * chunk, chunk)
                rows = pl.ds(r0, chunk)
                v = hi_ref[rows, :]
                o_inter = _nt_dot(qi_ref[rows, :].astype(BF16), st.astype(BF16))
                kabd = (tile_rows(ka_ref[rows, :]) * bdk).astype(BF16)
                amat = _nt_dot(qa_ref[rows, :].astype(BF16), kabd) * cmask
                vbd = (tile_rows(v) * bdk).astype(BF16)
                o_intra = jnp.dot(amat.astype(BF16), vbd, preferred_element_type=F32)
                oo_ref[rows, :] = o_inter + o_intra
                dst = _tn_dot(v.astype(BF16), ks_ref[rows, :].astype(BF16))
                el = jnp.exp(b_ref[pl.ds(r0 + chunk - 1, 1), :])
                st = st * el + dst * bds
            return st

        st_s[...] = lax.fori_loop(0, nblk // unroll, block_body, st_s[...])

        o = oo_ref[tile, :]
        sq = o * o
        hi = sq.astype(BF16)
        lo = (sq - hi.astype(F32)).astype(BF16)
        ms = jnp.dot(hi, seg, preferred_element_type=F32) + jnp.dot(lo, seg, preferred_element_type=F32)
        gz = hg_ref[tile, :]
        o_ref[tile, B_WIDTH:] = o * lax.rsqrt(ms + EPS) * ng_ref[...] * (gz * jax.nn.sigmoid(gz))

        @pl.when(t == pl.num_programs(1) - 1)
        def _():
            stt = st_s[...].T
            for h in range(C_HEADS):
                so_ref[s, h] = stt[h * C_DK:(h + 1) * C_DK, h * C_DV:(h + 1) * C_DV]


def _mix(p, row0, nseq, seq, tt, chunk, nsb, conv_state, s0, cw, lb, ng, consts):
    bdk, bds, cmask, seg = consts
    nt = seq // tt
    assert nseq % nsb == 0 and (nsb == 1 or nt == 1)
    rows = nsb * tt
    rb = row0 // rows

    def pspec(col):
        return pl.BlockSpec((rows, B_WIDTH), lambda s, t: (rb + s * nt + t, col // B_WIDTH))

    def const_spec(a):
        return pl.BlockSpec(a.shape, lambda s, t: (0,) * a.ndim)

    w = C_HEADS * C_DK
    lb = lb.reshape(1, w)
    ng = ng.reshape(1, C_WIDTH)
    return pl.pallas_call(
        functools.partial(_mix_kernel, chunk=chunk, nsb=nsb),
        grid=(nseq // nsb, nt),
        in_specs=[pspec(c) for c in (COL_BG, COL_CG, COL_BH, COL_HF, COL_HI, COL_HQ, COL_HG)] + [
            pl.BlockSpec((nsb, CONV_K - 1, B_WIDTH), lambda s, t: (s, 0, 0)),
            pl.BlockSpec((nsb, C_HEADS, C_DK, C_DV), lambda s, t: (s, 0, 0, 0)),
            const_spec(cw), const_spec(lb), const_spec(ng),
            const_spec(bdk), const_spec(bds), const_spec(cmask), const_spec(seg),
        ],
        out_specs=[
            pl.BlockSpec((rows, B_WIDTH + C_WIDTH), lambda s, t: (s * nt + t, 0)),
            pl.BlockSpec((nsb, CONV_K - 1, B_WIDTH), lambda s, t: (s, 0, 0)),
            pl.BlockSpec((nsb, C_HEADS, C_DK, C_DV), lambda s, t: (s, 0, 0, 0)),
        ],
        out_shape=[
            jax.ShapeDtypeStruct((nseq * seq, B_WIDTH + C_WIDTH), F32),
            jax.ShapeDtypeStruct((nseq, CONV_K - 1, B_WIDTH), F32),
            jax.ShapeDtypeStruct((nseq, C_HEADS, C_DK, C_DV), F32),
        ],
        scratch_shapes=[
            pltpu.VMEM((nsb, tt + SUBLANES, B_WIDTH), F32),
            pltpu.VMEM((nsb, C_WIDTH, w), F32),
            pltpu.VMEM((rows, w), F32),
            pltpu.VMEM((rows, w), F32),
            pltpu.VMEM((rows, w), F32),
            pltpu.VMEM((rows, w), F32),
            pltpu.VMEM((rows, w), F32),
            pltpu.VMEM((rows, C_WIDTH), F32),
        ],
        compiler_params=_cparams(("arbitrary", "arbitrary")),
        name="mix_c%d" % chunk,
    )(p, p, p, p, p, p, p, conv_state, s0, cw, lb, ng, bdk, bds, cmask, seg)


def _mix_consts(chunk):
    w = C_HEADS * C_DK
    rh = jnp.arange(C_HEADS * chunk)[:, None] // chunk
    ch = jnp.arange(w)[None, :] // C_DK
    bdk = (rh == ch).astype(F32)
    hv = jnp.arange(C_WIDTH)[:, None] // C_DV
    bds = (hv == ch).astype(F32)
    tq = jnp.arange(chunk)[:, None]
    sk = jnp.arange(C_HEADS * chunk)[None, :] % chunk
    cmask = (sk <= tq).astype(F32)
    seg = (bds / C_DV).astype(BF16)
    return bdk, bds, cmask, seg


def _t5_bucket(dist):
    dist = jnp.maximum(dist, 0)
    max_exact = N_BUCKETS // 2
    d = jnp.maximum(dist, 1).astype(F32)
    large = max_exact + (jnp.log(d / max_exact) / math.log(MAX_DISTANCE / max_exact)
                         * (N_BUCKETS - max_exact)).astype(I32)
    large = jnp.clip(large, 0, N_BUCKETS - 1)
    return jnp.where(dist < max_exact, dist, large)


def _bias_of_dist(rel_bias, dist):
    bucket = _t5_bucket(dist)
    table = rel_bias.astype(F32) * LOG2E
    expand = (1,) * dist.ndim
    bias = jnp.zeros((A_HEADS,) + dist.shape, F32)
    for b in range(N_BUCKETS):
        bias = jnp.where((bucket == b)[None], table[b].reshape((A_HEADS,) + expand), bias)
    return jnp.where((dist >= 0)[None], bias, NEG_BIG)


def _far_bias(rel_bias, min_dist):
    assert int(16 + math.log(min_dist / 16) / math.log(MAX_DISTANCE / 16) * 16) >= N_BUCKETS - 1
    return rel_bias.astype(F32)[N_BUCKETS - 1] * LOG2E


def _permute_w_in(w_in):
    sizes = (A_WIDTH, A_WIDTH, A_WIDTH, IDX_HEADS * IDX_DIM, IDX_DIM, IDX_HEADS,
             B_WIDTH, B_WIDTH, B_WIDTH, C_HEADS * C_DK, C_WIDTH, C_HEADS * C_DK, C_WIDTH)
    offs = [0]
    for s in sizes:
        offs.append(offs[-1] + s)
    cols = lambda i: w_in[..., offs[i]:offs[i + 1]]
    padw = LANES - IDX_DIM - IDX_HEADS
    pad = jnp.zeros(w_in.shape[:-1] + (padw,), w_in.dtype)
    parts = [cols(i) for i in (0, 1, 2, 3, 6, 7, 8, 9, 10, 11, 12)] + [cols(4), cols(5), pad]
    return jnp.concatenate(parts, axis=-1).astype(BF16)


def kernel(x_prompt, x_sample, cache_k, cache_v, cache_kidx, state_conv, state_hgrn, page_table,
           w_in, w_out, conv_w, hgrn_lb_logits, hgrn_norm_g, rel_bias, norm1_g, norm2_g,
           w_ff1, w_ff2, final_g):
    batch, seq, d_model = x_prompt.shape
    dec_batch, dec_seq, _ = x_sample.shape
    depth = w_in.shape[0]
    n_pool, page = cache_k.shape[1], cache_k.shape[2]
    n_pages = page_table.shape[1]
    past = n_pages * page
    n_p, n_s = batch * seq, dec_batch * dec_seq
    n = n_p + n_s
    assert dec_seq == SUBLANES and seq % (2 * TQ) == 0 and seq % SCORE_CHUNK == 0
    tm = 512 if n % 512 == 0 else 256
    assert n % tm == 0 and n_p % tm == 0

    x = jnp.concatenate([x_prompt.reshape(n_p, d_model), x_sample.reshape(n_s, d_model)], axis=0)
    w_in_p = _permute_w_in(w_in)
    w_out_b = w_out.astype(BF16)
    w_ff1_b = w_ff1.astype(BF16)
    w_ff2_b = w_ff2.astype(BF16)
    sm = jax.nn.softmax(hgrn_lb_logits.astype(F32), axis=0)
    lb_all = jnp.cumsum(sm, axis=0) - sm[0]

    ck = jnp.transpose(cache_k, (0, 1, 3, 4, 2))
    cv = jnp.transpose(cache_v, (0, 1, 3, 4, 2))
    cki = jnp.transpose(cache_kidx, (0, 1, 3, 2))

    qi_ = jnp.arange(TQ)[None, :]
    kj_ = jnp.arange(2 * TQ)[:, None]
    tab_p = jnp.stack([_bias_of_dist(rel_bias, qi_ - kj_), _bias_of_dist(rel_bias, TQ + qi_ - kj_)])
    cfar_p = _far_bias(rel_bias, TQ + 1)
    rows = A_HEADS * dec_seq
    si = jnp.arange(dec_seq)[:, None]
    sj = jnp.arange(LANES)[None, :]
    tab_last = _bias_of_dist(rel_bias, page + si - sj).reshape(rows, LANES)
    d_new = jnp.where(sj < dec_seq, si - sj, -1)
    tab_new = _bias_of_dist(rel_bias, d_new).reshape(rows, LANES)
    cfar_s = jnp.broadcast_to(jnp.repeat(_far_bias(rel_bias, page + 1), dec_seq)[:, None], (rows, LANES))
    bd_s = (jnp.arange(rows)[:, None] // dec_seq == jnp.arange(A_WIDTH)[None, :] // HEAD_DIM).astype(F32)
    tabs_s = (tab_last, tab_new, cfar_s, bd_s)

    chunk_p = math.gcd(seq, HGRN_CHUNK)
    chunk_s = math.gcd(dec_seq, HGRN_CHUNK)
    tt_p = 512 if seq % 512 == 0 else seq
    nsb_s = math.gcd(dec_batch, MIX_SEQS_PER_STEP)
    consts_p = _mix_consts(chunk_p)
    consts_s = _mix_consts(chunk_s)
    conv0 = jnp.zeros((batch, CONV_K - 1, B_WIDTH), F32)
    hgrn0 = jnp.zeros((batch, C_HEADS, C_DK, C_DV), F32)

    kbuf = jnp.zeros((depth, batch, A_WIDTH, seq), F32)
    vbuf = jnp.zeros((depth, batch, A_WIDTH, seq), F32)
    ibuf = jnp.zeros((depth, batch, IDX_DIM, seq), F32)

    ks, vs, kis, cps, sps, css, sss = [], [], [], [], [], [], []
    for l in range(depth):
        p, qt, qit, kb, vt, ikwb, iwt, kbuf, vbuf, ibuf = _inproj(
            x, norm1_g[l], w_in_p[l], tm, l, kbuf, vbuf, ibuf)
        a_p = _dsa_prompt(iwt, qt, qit, kb, ikwb, vt, tab_p, cfar_p, batch, seq)
        a_s = _dsa_sample(p, n_p, cki, ck, cv, l, page_table, tabs_s, dec_batch, dec_seq)
        bc_p, conv_p, s_p = _mix(p, 0, batch, seq, tt_p, chunk_p, 1, conv0, hgrn0,
                                 conv_w[l], lb_all[l], hgrn_norm_g[l], consts_p)
        bc_s, conv_s, s_s = _mix(p, n_p, dec_batch, dec_seq, dec_seq, chunk_s, nsb_s, state_conv[l],
                                 state_hgrn[l], conv_w[l], lb_all[l], hgrn_norm_g[l], consts_s)
        x = _outffn(x, a_p, bc_p, a_s, bc_s, w_out_b[l], norm2_g[l], w_ff1_b[l], w_ff2_b[l], final_g,
                    tm, 1024, l == depth - 1)
        ks.append(p[n_p:, COL_K:COL_K + A_WIDTH])
        vs.append(p[n_p:, COL_V:COL_V + A_WIDTH])
        kis.append(p[n_p:, COL_IKW:COL_IKW + IDX_DIM])
        cps.append(conv_p); sps.append(s_p); css.append(conv_s); sss.append(s_s)

    heads = lambda buf: jnp.transpose(buf.reshape(depth, batch, A_HEADS, HEAD_DIM, seq), (0, 1, 4, 2, 3))
    return (
        x[:n_p].reshape(batch, seq, d_model),
        x[n_p:].reshape(dec_batch, dec_seq, d_model),
        heads(kbuf), heads(vbuf), jnp.transpose(ibuf, (0, 1, 3, 2)),
        jnp.stack(cps), jnp.stack(sps),
        jnp.stack(ks).reshape(depth, dec_batch, dec_seq, A_HEADS, HEAD_DIM),
        jnp.stack(vs).reshape(depth, dec_batch, dec_seq, A_HEADS, HEAD_DIM),
        jnp.stack(kis).reshape(depth, dec_batch, dec_seq, IDX_DIM),
        jnp.stack(css), jnp.stack(sss),
    )
```
